```python
import math
import jax
import jax.numpy as jnp
from jax import lax

D_MODEL = 1024
BATCH = 2
SEQ = 8192
DEPTH = 4
DEC_BATCH = 32
DEC_SEQ = 1
PAST_LEN = 8192
PAGE_SIZE = 128

N_MIXERS = 3
N_A = (DEPTH + 2) // 3
N_B = (DEPTH + 1) // 3
N_C = DEPTH // 3
MIX_W = D_MODEL
MEM_TOKENS = 256
MEM_HEADS = 4
MEM_HD = D_MODEL // 8
MEM_W = MEM_HEADS * MEM_HD
GATE_TAIL = MIX_W + 2 * MEM_W
OUT_W = MIX_W + MEM_W
DA_HEADS = 8
DA_HD = MIX_W // (2 * DA_HEADS)
ROT_DIM = DA_HD // 4
ROPE_THETA = 500000.0
Q_BLOCK = 128
ATTN_IN = 3 * MIX_W + GATE_TAIL
GM_CHUNK = 128
GM_GROUPS = 8
GM_GD = MIX_W // GM_GROUPS
GM_IN = 2 * MIX_W + GATE_TAIL
SSD_HD = 64
SSD_HEADS = MIX_W // SSD_HD
SSD_GROUPS = 4
SSD_N = 128
SSD_CONV = 4
SSD_CHUNK = 128
SSD_CONV_DIM = MIX_W + 2 * SSD_GROUPS * SSD_N
SSD_IN = SSD_CONV_DIM + SSD_HEADS + GATE_TAIL
EPS = 1e-6

kernel_name = 'hybrid_diffattn_gmlp_ssd_decoder_step'


def rmsnorm(x, g):
    xf = x.astype(jnp.float32)
    y = xf * lax.rsqrt(jnp.mean(xf * xf, axis=-1, keepdims=True) + EPS)
    return (y * g.astype(jnp.float32)).astype(x.dtype)


def layernorm(x, g):
    xf = x.astype(jnp.float32)
    xc = xf - jnp.mean(xf, axis=-1, keepdims=True)
    y = xc * lax.rsqrt(jnp.mean(xc * xc, axis=-1, keepdims=True) + EPS)
    return (y * g.astype(jnp.float32)).astype(x.dtype)


def rope(x, pos):
    half = ROT_DIM // 2
    inv = ROPE_THETA ** (-jnp.arange(half, dtype=jnp.float32) / half)
    ang = pos.astype(jnp.float32)[:, None] * inv[None, :]
    shape = (1, pos.shape[0]) + (1,) * (x.ndim - 3) + (half,)
    cos = jnp.cos(ang).reshape(shape).astype(x.dtype)
    sin = jnp.sin(ang).reshape(shape).astype(x.dtype)
    x1 = x[..., :half]
    x2 = x[..., half:ROT_DIM]
    return jnp.concatenate([x1 * cos - x2 * sin, x2 * cos + x1 * sin, x[..., ROT_DIM:]], axis=-1)


def split_proj(p, n_main):
    return (p[..., :n_main], p[..., n_main:n_main + MIX_W],
            p[..., n_main + MIX_W:n_main + MIX_W + MEM_W], p[..., n_main + MIX_W + MEM_W:])


def attn_qkv(main, pos):
    b, t, _ = main.shape
    q = main[..., :MIX_W].reshape(b, t, DA_HEADS, 2, DA_HD)
    k = main[..., MIX_W:2 * MIX_W].reshape(b, t, DA_HEADS, 2, DA_HD)
    v = main[..., 2 * MIX_W:].reshape(b, t, DA_HEADS, 2 * DA_HD)
    return rope(q, pos), rope(k, pos), v


def diff_lambda(lp, lam_init):
    lf = lp.astype(jnp.float32)
    return jnp.exp(jnp.sum(lf[0] * lf[1])) - jnp.exp(jnp.sum(lf[2] * lf[3])) + lam_init


def diff_attend(q, k, v, qpos, kpos, lam):
    s = jnp.einsum('bthjd,bshjd->bjhts', q, k).astype(jnp.float32) * (DA_HD ** -0.5)
    mask = kpos[None, :] <= qpos[:, None]
    p = jax.nn.softmax(jnp.where(mask, s, -jnp.inf), axis=-1)
    w = p[:, 0] - lam * p[:, 1]
    return jnp.einsum('bhts,bshe->bthe', w.astype(v.dtype), v)


def diff_attend_blocked(q, k, v, pos, lam):
    b, t = q.shape[0], q.shape[1]
    nb = t // Q_BLOCK
    qb = jnp.moveaxis(q.reshape(b, nb, Q_BLOCK, DA_HEADS, 2, DA_HD), 1, 0)
    pb = pos.reshape(nb, Q_BLOCK)
    o = lax.map(lambda a: diff_attend(a[0], k, v, a[1], pos, lam), (qb, pb))
    return jnp.moveaxis(o, 0, 1).reshape(b, t, DA_HEADS, 2 * DA_HD)


def diff_out(o, g, lam_init):
    b, t = o.shape[0], o.shape[1]
    return (rmsnorm(o, g) * (1.0 - lam_init)).reshape(b, t, MIX_W)


def gm_mixer(main, norm_g, ws, bias):
    b, t, _ = main.shape
    q = min(GM_CHUNK, t)
    nc = t // q
    uv = jax.nn.gelu(main)
    u = uv[..., :MIX_W]
    vn = layernorm(uv[..., MIX_W:], norm_g)
    tri = jnp.tril(jnp.ones((q, q), dtype=bool))
    wm = jnp.where(tri, ws[:, :q, :q], 0.0).astype(vn.dtype)
    sv = jnp.einsum('gts,bcsgd->bctgd', wm, vn.reshape(b, nc, q, GM_GROUPS, GM_GD))
    sv = sv + bias[:, :q].T[:, :, None].astype(vn.dtype)
    return u * sv.reshape(b, t, MIX_W), vn


def causal_conv(x, buf, w, bias):
    t = x.shape[1]
    full = jnp.concatenate([buf.astype(x.dtype), x], axis=1)
    out = sum(full[:, k:k + t] * w[k] for k in range(SSD_CONV)) + bias
    return out, full[:, t:]


def ssd_chunked(x, dt, a, bm, cm, s0):
    b, t, h, p = x.shape
    q = min(SSD_CHUNK, t)
    nc = t // q
    g, n = bm.shape[2], bm.shape[3]
    r = h // g
    f32 = jnp.float32
    xc = x.reshape(b, nc, q, g, r, p).astype(f32)
    bc = bm.reshape(b, nc, q, g, n).astype(f32)
    cc = cm.reshape(b, nc, q, g, n).astype(f32)
    dtc = dt.reshape(b, nc, q, g, r)
    cum = jnp.cumsum(dtc * a.reshape(g, r), axis=2)
    tri = jnp.tril(jnp.ones((q, q), dtype=bool))[:, :, None, None]
    seg = cum[:, :, :, None] - cum[:, :, None, :]
    decay = jnp.exp(jnp.where(tri, seg, -jnp.inf))
    cb = jnp.einsum('bctgn,bcsgn->bctsg', cc, bc)
    w = cb[..., None] * decay * dtc[:, :, None]
    y_diag = jnp.einsum('bctsgr,bcsgrp->bctgrp', w, xc)
    w_end = jnp.exp(cum[:, :, -1:] - cum) * dtc
    states = jnp.einsum('bcsgn,bcsgr,bcsgrp->bcgrpn', bc, w_end, xc)
    chunk_decay = jnp.exp(cum[:, :, -1])

    def step(s, inp):
        st, dec = inp
        return s * dec[..., None, None] + st, s

    s_fin, s_in = lax.scan(step, s0.astype(f32).reshape(b, g, r, p, n),
                           (jnp.moveaxis(states, 1, 0), jnp.moveaxis(chunk_decay, 1, 0)))
    s_in = jnp.moveaxis(s_in, 0, 1)
    y_off = jnp.einsum('bctgn,bcgrpn->bctgrp', cc, s_in) * jnp.exp(cum)[..., None]
    y = (y_diag + y_off).reshape(b, t, h, p).astype(x.dtype)
    return y, s_fin.reshape(b, h, p, n).astype(s0.dtype)


def ssd_mixer(main, z, conv_buf, s0, conv_w, conv_b, dt_bias, a_log, d_skip, norm_g):
    b, t, _ = main.shape
    xbc, new_buf = causal_conv(main[..., :SSD_CONV_DIM], conv_buf, conv_w, conv_b)
    xbc = jax.nn.silu(xbc)
    xs = xbc[..., :MIX_W].reshape(b, t, SSD_HEADS, SSD_HD)
    bm = xbc[..., MIX_W:MIX_W + SSD_GROUPS * SSD_N].reshape(b, t, SSD_GROUPS, SSD_N)
    cm = xbc[..., MIX_W + SSD_GROUPS * SSD_N:].reshape(b, t, SSD_GROUPS, SSD_N)
    dt = jax.nn.softplus(main[..., SSD_CONV_DIM:].astype(jnp.float32) + dt_bias.astype(jnp.float32))
    a = -jnp.exp(a_log.astype(jnp.float32))
    y, s_fin = ssd_chunked(xs, dt, a, bm, cm, s0)
    y = y + d_skip[:, None] * xs
    y = y.reshape(b, t, MIX_W) * jax.nn.silu(z)
    y = rmsnorm(y.reshape(b, t, SSD_GROUPS, MIX_W // SSD_GROUPS), norm_g.reshape(SSD_GROUPS, -1))
    return y.reshape(b, t, MIX_W), new_buf, s_fin


def mem_attend(q, mk, mv):
    b, t, _ = q.shape
    q = q.reshape(b, t, MEM_HEADS, MEM_HD)
    s = jnp.einsum('bthd,bmhd->bhtm', q, mk).astype(jnp.float32) * (MEM_HD ** -0.5)
    p = jax.nn.softmax(s, axis=-1).astype(mv.dtype)
    return jnp.einsum('bhtm,bmhd->bthd', p, mv).reshape(b, t, MEM_W)


def finish(h, mo, mq, mg, mk, mv, w_o, g_post):
    ma = jax.nn.silu(mg) * mem_attend(mq, mk, mv)
    out = jnp.concatenate([mo, ma], axis=-1) @ w_o
    return h + rmsnorm(out, g_post)


def setup_inputs(seed: int = 0) -> dict:
    key = jax.random.key(seed)
    keys = jax.random.split(key, 40)
    counter = iter(range(40))
    f32 = jnp.float32

    def nk():
        return keys[next(counter)]

    def nrm(shape, scale=1.0):
        return jax.random.normal(nk(), shape, f32) * scale

    def gain(shape):
        return 1.0 + nrm(shape, 0.05)

    n_pages = PAST_LEN // PAGE_SIZE
    n_used = DEC_BATCH * n_pages
    n_pool = n_used + max(1, n_used // 4)
    page_table = jax.random.permutation(nk(), n_pool)[:n_used].reshape(DEC_BATCH, n_pages).astype(jnp.int32)
    dt0 = jnp.exp(jax.random.uniform(nk(), (N_C, SSD_HEADS), f32, math.log(1e-3), math.log(1e-1)))
    a_init = jax.random.uniform(nk(), (N_C, SSD_HEADS), f32, 1.0, 16.0)
    d_in = D_MODEL ** -0.5
    return {
        'x_prompt': nrm((BATCH, SEQ, D_MODEL)),
        'x_sample': nrm((DEC_BATCH, DEC_SEQ, D_MODEL)),
        'cache_attn_k': nrm((n_pool, PAGE_SIZE, N_A, DA_HEADS, 2 * DA_HD)),
        'cache_attn_v': nrm((n_pool, PAGE_SIZE, N_A, DA_HEADS, 2 * DA_HD)),
        'cache_mem_k': nrm((DEC_BATCH, MEM_TOKENS, DEPTH, MEM_HEADS, MEM_HD)),
        'cache_mem_v': nrm((DEC_BATCH, MEM_TOKENS, DEPTH, MEM_HEADS, MEM_HD)),
        'state_ssm': nrm((N_C, DEC_BATCH, SSD_HEADS, SSD_HD, SSD_N), 0.5),
        'state_conv': nrm((N_C, DEC_BATCH, SSD_CONV - 1, SSD_CONV_DIM)),
        'page_table': page_table,
        'mem_prompt': nrm((BATCH, MEM_TOKENS, D_MODEL)),
        'norm_pre': gain((DEPTH, D_MODEL)),
        'norm_post': gain((DEPTH, D_MODEL)),
        'w_mem_kv': nrm((DEPTH, D_MODEL, 2 * MEM_W), d_in),
        'w_out': nrm((DEPTH, OUT_W, D_MODEL), OUT_W ** -0.5),
        'attn_w_in': nrm((N_A, D_MODEL, ATTN_IN), d_in),
        'attn_lambda': nrm((N_A, 4, DA_HD), 0.1),
        'attn_subln': gain((N_A, 2 * DA_HD)),
        'gm_w_in': nrm((N_B, D_MODEL, GM_IN), d_in),
        'gm_norm': gain((N_B, MIX_W)),
        'gm_ws': nrm((N_B, GM_GROUPS, GM_CHUNK, GM_CHUNK), GM_CHUNK ** -0.5),
        'gm_bias': gain((N_B, GM_GROUPS, GM_CHUNK)),
        'ssd_w_in': nrm((N_C, D_MODEL, SSD_IN), d_in),
        'ssd_conv_w': nrm((N_C, SSD_CONV, SSD_CONV_DIM), 0.5),
        'ssd_conv_b': nrm((N_C, SSD_CONV_DIM), 0.05),
        'ssd_dt_bias': dt0 + jnp.log(-jnp.expm1(-dt0)),
        'ssd_a_log': jnp.log(a_init),
        'ssd_d': gain((N_C, SSD_HEADS)),
        'ssd_norm': gain((N_C, MIX_W)),
    }


def reference(x_prompt, x_sample, cache_attn_k, cache_attn_v, cache_mem_k, cache_mem_v,
              state_ssm, state_conv, page_table, mem_prompt,
              norm_pre, norm_post, w_mem_kv, w_out,
              attn_w_in, attn_lambda, attn_subln,
              gm_w_in, gm_norm, gm_ws, gm_bias,
              ssd_w_in, ssd_conv_w, ssd_conv_b, ssd_dt_bias, ssd_a_log, ssd_d, ssd_norm):
    bp, sp, _ = x_prompt.shape
    bs, ts, _ = x_sample.shape
    n_mem = mem_prompt.shape[1]
    past = page_table.shape[1] * PAGE_SIZE
    pos_p = jnp.arange(sp, dtype=jnp.int32)
    pos_s = past + jnp.arange(ts, dtype=jnp.int32)
    kpos_s = jnp.arange(past + ts, dtype=jnp.int32)
    hp, hs = x_prompt, x_sample
    k_p, v_p, k_s, v_s = [], [], [], []
    mk_list, mv_list = [], []
    ssm_p, conv_p, ssm_s, conv_s = [], [], [], []
    gm_v_s = []
    for i in range(DEPTH):
        kind, j = i % N_MIXERS, i // N_MIXERS
        w_in = (attn_w_in, gm_w_in, ssd_w_in)[kind][j]
        n_main = w_in.shape[1] - GATE_TAIL
        main_p, gate_p, mq_p, mg_p = split_proj(rmsnorm(hp, norm_pre[i]) @ w_in, n_main)
        main_s, gate_s, mq_s, mg_s = split_proj(rmsnorm(hs, norm_pre[i]) @ w_in, n_main)
        if kind == 0:
            lam_init = 0.8 - 0.6 * math.exp(-0.3 * i)
            lam = diff_lambda(attn_lambda[j], lam_init)
            q, k, v = attn_qkv(main_p, pos_p)
            o_p = diff_attend_blocked(q, k, v, pos_p, lam)
            k_p.append(k.reshape(bp, sp, DA_HEADS, 2 * DA_HD))
            v_p.append(v)
            q, k, v = attn_qkv(main_s, pos_s)
            k_past = cache_attn_k[page_table, :, j].reshape(bs, past, DA_HEADS, 2, DA_HD)
            v_past = cache_attn_v[page_table, :, j].reshape(bs, past, DA_HEADS, 2 * DA_HD)
            o_s = diff_attend(q, jnp.concatenate([k_past, k.astype(k_past.dtype)], axis=1),
                              jnp.concatenate([v_past, v.astype(v_past.dtype)], axis=1), pos_s, kpos_s, lam)
            k_s.append(k.reshape(bs, ts, DA_HEADS, 2 * DA_HD))
            v_s.append(v)
            mo_p = jax.nn.silu(gate_p) * diff_out(o_p, attn_subln[j], lam_init)
            mo_s = jax.nn.silu(gate_s) * diff_out(o_s, attn_subln[j], lam_init)
        elif kind == 1:
            y_p, _ = gm_mixer(main_p, gm_norm[j], gm_ws[j], gm_bias[j])
            y_s, vn_s = gm_mixer(main_s, gm_norm[j], gm_ws[j], gm_bias[j])
            mo_p = jax.nn.silu(gate_p) * y_p
            mo_s = jax.nn.silu(gate_s) * y_s
            gm_v_s.append(vn_s)
        else:
            buf0 = jnp.zeros((bp, SSD_CONV - 1, SSD_CONV_DIM), x_prompt.dtype)
            st0 = jnp.zeros((bp, SSD_HEADS, SSD_HD, SSD_N), x_prompt.dtype)
            mo_p, cb_p, sf_p = ssd_mixer(main_p, gate_p, buf0, st0, ssd_conv_w[j], ssd_conv_b[j],
                                         ssd_dt_bias[j], ssd_a_log[j], ssd_d[j], ssd_norm[j])
            mo_s, cb_s, sf_s = ssd_mixer(main_s, gate_s, state_conv[j], state_ssm[j], ssd_conv_w[j], ssd_conv_b[j],
                                         ssd_dt_bias[j], ssd_a_log[j], ssd_d[j], ssd_norm[j])
            ssm_p.append(sf_p)
            conv_p.append(cb_p)
            ssm_s.append(sf_s)
            conv_s.append(cb_s)
        mkv = jnp.einsum('bmd,de->bme', mem_prompt, w_mem_kv[i]).reshape(bp, n_mem, 2, MEM_HEADS, MEM_HD)
        mk_list.append(mkv[:, :, 0])
        mv_list.append(mkv[:, :, 1])
        hp = finish(hp, mo_p, mq_p, mg_p, mkv[:, :, 0], mkv[:, :, 1], w_out[i], norm_post[i])
        hs = finish(hs, mo_s, mq_s, mg_s, cache_mem_k[:, :, i], cache_mem_v[:, :, i], w_out[i], norm_post[i])
    k_prompt = jnp.stack(k_p, axis=2)
    v_prompt = jnp.stack(v_p, axis=2)
    mem_k_prompt = jnp.stack(mk_list, axis=2)
    mem_v_prompt = jnp.stack(mv_list, axis=2)
    ssm_prompt = jnp.stack(ssm_p, axis=0)
    conv_prompt = jnp.stack(conv_p, axis=0)
    k_sample = jnp.stack(k_s, axis=2)
    v_sample = jnp.stack(v_s, axis=2)
    ssm_sample = jnp.stack(ssm_s, axis=0)
    conv_sample = jnp.stack(conv_s, axis=0)
    gm_v_sample = jnp.stack(gm_v_s, axis=2)
    return (hp, hs, k_prompt, v_prompt, mem_k_prompt, mem_v_prompt, ssm_prompt, conv_prompt,
            k_sample, v_sample, ssm_sample, conv_sample, gm_v_sample)
```

```python
import functools
import math

import jax
import jax.numpy as jnp
from jax import lax
from jax.experimental import pallas as pl
from jax.experimental.pallas import tpu as pltpu

F32 = jnp.float32
BF16 = jnp.bfloat16

D_MODEL = 1024
MIX_W = D_MODEL
MEM_HEADS = 4
MEM_HD = 128
MEM_W = MEM_HEADS * MEM_HD
DA_HEADS = 8
DA_HD = 64
HEAD_W = 2 * DA_HD
ROT_DIM = DA_HD // 4
ROT_HALF = ROT_DIM // 2
ROPE_THETA = 500000.0
GM_CHUNK = 128
GM_GROUPS = 8
SSD_HD = 64
SSD_HEADS = MIX_W // SSD_HD
SSD_GROUPS = 4
SSD_N = 128
SSD_CONV = 4
SSD_CHUNK = 128
SSD_PAIRS = SSD_HEADS // 2
SSD_CONV_DIM = MIX_W + 2 * SSD_GROUPS * SSD_N
EPS = 1e-6
NEG_BIG = -1e30
LANES = 128
VMEM_LIMIT = 56 * 1024 * 1024

_NT = (((1,), (1,)), ((), ()))


def _params(*sem):
    return pltpu.CompilerParams(dimension_semantics=sem, vmem_limit_bytes=VMEM_LIMIT)


def _dot(a, b):
    return jnp.dot(a, b, preferred_element_type=F32)


def _dot_nt(a, b):
    return lax.dot_general(a, b, _NT, preferred_element_type=F32)


def _rms_scale(x, g):
    return x * lax.rsqrt(jnp.mean(x * x, axis=-1, keepdims=True) + EPS) * g


def _rope_head(x, c, s):
    lane = lax.broadcasted_iota(jnp.int32, x.shape, 1) & (DA_HD - 1)
    partner = jnp.where(lane < ROT_HALF, pltpu.roll(x, LANES - ROT_HALF, 1), pltpu.roll(x, ROT_HALF, 1))
    return x * c + partner * s


def _inproj_attn_kernel(x_ref, g_ref, w_ref, c_ref, s_ref,
                        q_ref, k_ref, v_ref, kb_ref, vb_ref, gate_ref, mq_ref, mg_ref):
    xn = _rms_scale(x_ref[...], g_ref[...]).astype(BF16)
    c = c_ref[...]
    s = s_ref[...]
    q = _dot(xn, w_ref[:, 0:MIX_W])
    k = _dot(xn, w_ref[:, MIX_W:2 * MIX_W])
    for h in range(DA_HEADS):
        sl = slice(h * HEAD_W, (h + 1) * HEAD_W)
        q_ref[:, sl] = (_rope_head(q[:, sl], c, s) * (DA_HD ** -0.5)).astype(q_ref.dtype)
        kr = _rope_head(k[:, sl], c, s)
        k_ref[:, sl] = kr
        kb_ref[:, sl] = kr.astype(BF16)
    v = _dot(xn, w_ref[:, 2 * MIX_W:3 * MIX_W])
    v_ref[...] = v
    vb_ref[...] = v.astype(BF16)
    o = 3 * MIX_W
    gate_ref[...] = _dot(xn, w_ref[:, o:o + MIX_W])
    mq_ref[...] = _dot(xn, w_ref[:, o + MIX_W:o + MIX_W + MEM_W]).astype(mq_ref.dtype)
    mg_ref[...] = _dot(xn, w_ref[:, o + MIX_W + MEM_W:o + MIX_W + 2 * MEM_W])


def _inproj_attn(x, g, w, cos_t, sin_t, tm, q_dtype, mq_dtype):
    t = x.shape[0]
    n_pos_blocks = cos_t.shape[0] // tm
    row = lambda i: (i, 0)
    fixed = lambda i: (0, 0)
    pos = lambda i: (i % n_pos_blocks, 0)
    wide = pl.BlockSpec((tm, MIX_W), row)
    mem = pl.BlockSpec((tm, MEM_W), row)
    return pl.pallas_call(
        _inproj_attn_kernel,
        grid=(t // tm,),
        in_specs=[wide, pl.BlockSpec((1, D_MODEL), fixed), pl.BlockSpec(w.shape, fixed),
                  pl.BlockSpec((tm, HEAD_W), pos), pl.BlockSpec((tm, HEAD_W), pos)],
        out_specs=[wide, wide, wide, wide, wide, wide, mem, mem],
        out_shape=[jax.ShapeDtypeStruct((t, MIX_W), q_dtype),
                   jax.ShapeDtypeStruct((t, MIX_W), F32), jax.ShapeDtypeStruct((t, MIX_W), F32),
                   jax.ShapeDtypeStruct((t, MIX_W), BF16), jax.ShapeDtypeStruct((t, MIX_W), BF16),
                   jax.ShapeDtypeStruct((t, MIX_W), F32),
                   jax.ShapeDtypeStruct((t, MEM_W), mq_dtype), jax.ShapeDtypeStruct((t, MEM_W), F32)],
        compiler_params=_params("parallel"),
        name="inproj_attn",
    )(x, g, w, cos_t, sin_t)


def _inproj_plain_kernel(x_ref, g_ref, w_ref, *out_refs, starts):
    xn = _rms_scale(x_ref[...], g_ref[...]).astype(BF16)
    for ref, start in zip(out_refs, starts):
        ref[...] = _dot(xn, w_ref[:, start:start + ref.shape[1]]).astype(ref.dtype)


def _inproj_plain(x, g, w, tm, segments, name):
    t = x.shape[0]
    starts, acc = [], 0
    for width, _ in segments:
        starts.append(acc)
        acc += width
    assert acc == w.shape[1]
    row = lambda i: (i, 0)
    fixed = lambda i: (0, 0)
    return pl.pallas_call(
        functools.partial(_inproj_plain_kernel, starts=tuple(starts)),
        grid=(t // tm,),
        in_specs=[pl.BlockSpec((tm, D_MODEL), row), pl.BlockSpec((1, D_MODEL), fixed),
                  pl.BlockSpec(w.shape, fixed)],
        out_specs=[pl.BlockSpec((tm, width), row) for width, _ in segments],
        out_shape=[jax.ShapeDtypeStruct((t, width), dt) for width, dt in segments],
        compiler_params=_params("parallel"),
        name=name,
    )(x, g, w)


def _memkv_kernel(x_ref, w_ref, mk_ref, mv_ref, mkb_ref, mvb_ref):
    kv = _dot(x_ref[...].astype(BF16), w_ref[0].astype(BF16))
    k = kv[:, :MEM_W]
    v = kv[:, MEM_W:]
    mk_ref[...] = k
    mv_ref[...] = v
    mkb_ref[0] = k.astype(BF16)
    mvb_ref[0] = v.astype(BF16)


def _memkv(mem_tokens, w_mem_kv):
    tm = mem_tokens.shape[0]
    depth = w_mem_kv.shape[0]
    f32_out = pl.BlockSpec((tm, MEM_W), lambda i: (0, i))
    bf_out = pl.BlockSpec((1, tm, MEM_W), lambda i: (i, 0, 0))
    return pl.pallas_call(
        _memkv_kernel,
        grid=(depth,),
        in_specs=[pl.BlockSpec((tm, D_MODEL), lambda i: (0, 0)),
                  pl.BlockSpec((1, D_MODEL, 2 * MEM_W), lambda i: (i, 0, 0))],
        out_specs=[f32_out, f32_out, bf_out, bf_out],
        out_shape=[jax.ShapeDtypeStruct((tm, depth * MEM_W), F32), jax.ShapeDtypeStruct((tm, depth * MEM_W), F32),
                   jax.ShapeDtypeStruct((depth, tm, MEM_W), BF16), jax.ShapeDtypeStruct((depth, tm, MEM_W), BF16)],
        compiler_params=_params("parallel"),
        name="memkv",
    )(mem_tokens, w_mem_kv)


def _diff_lambda(lp, lam_init):
    a = jnp.sum(lp[0:1, :] * lp[1:2, :], axis=-1, keepdims=True)
    b = jnp.sum(lp[2:3, :] * lp[3:4, :], axis=-1, keepdims=True)
    return jnp.exp(a) - jnp.exp(b) + lam_init


def _sub_norm(o, g, lam_init):
    return _rms_scale(o, g) * (1.0 - lam_init)


def _flash_kernel(lam_ref, q_ref, k_ref, v_ref, g_ref, o_ref, m_sc, l_sc, acc_sc, *, blk, lam_init):
    i = pl.program_id(2)
    q = q_ref[0]
    lane = lax.broadcasted_iota(jnp.int32, q.shape, 1)
    zero = jnp.zeros_like(q)
    qq = jnp.concatenate([jnp.where(lane < DA_HD, q, zero), jnp.where(lane >= DA_HD, q, zero)], axis=0)
    m_sc[...] = jnp.full(m_sc.shape, NEG_BIG, F32)
    l_sc[...] = jnp.zeros(l_sc.shape, F32)
    acc_sc[...] = jnp.zeros(acc_sc.shape, F32)

    def step(j, masked):
        start = pl.multiple_of(j * blk, blk)
        k = k_ref[0, pl.ds(start, blk), :]
        v = v_ref[0, pl.ds(start, blk), :]
        s = _dot_nt(qq, k)
        if masked:
            r = lax.broadcasted_iota(jnp.int32, (blk, blk), 0)
            c = lax.broadcasted_iota(jnp.int32, (blk, blk), 1)
            keep = jnp.concatenate([c <= r, c <= r], axis=0)
            s = jnp.where(keep, s, NEG_BIG)
        m_old = m_sc[...]
        m_new = jnp.maximum(m_old, jnp.max(s, axis=-1, keepdims=True))
        alpha = jnp.exp(m_old - m_new)
        p = jnp.exp(s - m_new)
        l_sc[...] = alpha * l_sc[...] + jnp.sum(p, axis=-1, keepdims=True)
        acc_sc[...] = alpha * acc_sc[...] + _dot(p.astype(BF16), v)
        m_sc[...] = m_new

    def body(j, carry):
        step(j, False)
        return carry

    lax.fori_loop(0, i, body, 0)
    step(i, True)

    o = acc_sc[...] / l_sc[...]
    lam = _diff_lambda(lam_ref[...], lam_init)
    od = o[:blk] - lam * o[blk:]
    o_ref[0] = _sub_norm(od, g_ref[...], lam_init)


def _flash(lam_p, q, kb, vb, subln, blk, lam_init):
    b, s, _ = q.shape
    qo = pl.BlockSpec((1, blk, HEAD_W), lambda bi, h, i: (bi, i, h))
    kv = pl.BlockSpec((1, s, HEAD_W), lambda bi, h, i: (bi, 0, h))
    return pl.pallas_call(
        functools.partial(_flash_kernel, blk=blk, lam_init=lam_init),
        grid=(b, DA_HEADS, s // blk),
        in_specs=[pl.BlockSpec((4, DA_HD), lambda bi, h, i: (0, 0)), qo, kv, kv,
                  pl.BlockSpec((1, HEAD_W), lambda bi, h, i: (0, 0))],
        out_specs=qo,
        out_shape=jax.ShapeDtypeStruct((b, s, MIX_W), F32),
        scratch_shapes=[pltpu.VMEM((2 * blk, 1), F32), pltpu.VMEM((2 * blk, 1), F32),
                        pltpu.VMEM((2 * blk, HEAD_W), F32)],
        compiler_params=_params("parallel", "parallel", "arbitrary"),
        name="flash_diff",
    )(lam_p, q, kb, vb, subln)


def _decode_kernel(pt_ref, lam_ref, q_ref, kn_ref, vn_ref, kc_ref, vc_ref, g_ref, o_ref,
                   m_sc, l_sc, acc_sc, *, n_pages, lam_init):
    del pt_ref
    p = pl.program_id(1)
    q8 = q_ref[0]
    lane = lax.broadcasted_iota(jnp.int32, q8.shape, 1)
    zero = jnp.zeros_like(q8)
    qsel = jnp.concatenate([jnp.where(lane < DA_HD, q8, zero), jnp.where(lane >= DA_HD, q8, zero)],
                           axis=0).astype(BF16)

    @pl.when(p == 0)
    def _():
        m_sc[...] = jnp.full(m_sc.shape, NEG_BIG, F32)
        l_sc[...] = jnp.zeros(l_sc.shape, F32)
        acc_sc[...] = jnp.zeros(acc_sc.shape, F32)

    page = kc_ref.shape[1]
    rows = page * DA_HEADS
    k2 = kc_ref[0, :, 0].reshape(rows, HEAD_W).astype(BF16)
    v2 = vc_ref[0, :, 0].reshape(rows, HEAD_W).astype(BF16)
    s = _dot_nt(qsel, k2)
    col = lax.broadcasted_iota(jnp.int32, s.shape, 1)
    row = lax.broadcasted_iota(jnp.int32, s.shape, 0)
    s = jnp.where((col & (DA_HEADS - 1)) == (row & (DA_HEADS - 1)), s, NEG_BIG)
    m_old = m_sc[...]
    m_new = jnp.maximum(m_old, jnp.max(s, axis=-1, keepdims=True))
    alpha = jnp.exp(m_old - m_new)
    pm = jnp.exp(s - m_new)
    l_new = alpha * l_sc[...] + jnp.sum(pm, axis=-1, keepdims=True)
    acc_new = alpha * acc_sc[...] + _dot(pm.astype(BF16), v2)
    m_sc[...] = m_new
    l_sc[...] = l_new
    acc_sc[...] = acc_new

    @pl.when(p == n_pages - 1)
    def _():
        kn = kn_ref[0].astype(BF16).astype(F32)
        vn = vn_ref[0].astype(BF16).astype(F32)
        kn2 = jnp.concatenate([kn, kn], axis=0)
        vn2 = jnp.concatenate([vn, vn], axis=0)
        s_n = jnp.sum(qsel.astype(F32) * kn2, axis=-1, keepdims=True)
        m2 = jnp.maximum(m_new, s_n)
        a2 = jnp.exp(m_new - m2)
        pn = jnp.exp(s_n - m2)
        l2 = a2 * l_new + pn
        acc2 = a2 * acc_new + pn.astype(BF16).astype(F32) * vn2
        o = acc2 / l2
        lam = _diff_lambda(lam_ref[...], lam_init)
        od = o[:DA_HEADS] - lam * o[DA_HEADS:]
        o_ref[0] = _sub_norm(od, g_ref[...], lam_init)


def _decode_attn(page_table, lam_p, q, k_new, v_new, cache_k, cache_v, subln, layer, lam_init):
    bs, n_pages = page_table.shape
    page = cache_k.shape[1]
    per_seq = pl.BlockSpec((1, DA_HEADS, HEAD_W), lambda b, p, pt: (b, 0, 0))
    cache = pl.BlockSpec((1, page, 1, DA_HEADS, HEAD_W), lambda b, p, pt: (pt[b * n_pages + p], 0, layer, 0, 0))
    return pl.pallas_call(
        functools.partial(_decode_kernel, n_pages=n_pages, lam_init=lam_init),
        grid_spec=pltpu.PrefetchScalarGridSpec(
            num_scalar_prefetch=1,
            grid=(bs, n_pages),
            in_specs=[pl.BlockSpec((4, DA_HD), lambda b, p, pt: (0, 0)), per_seq, per_seq, per_seq, cache, cache,
                      pl.BlockSpec((1, HEAD_W), lambda b, p, pt: (0, 0))],
            out_specs=per_seq,
            scratch_shapes=[pltpu.VMEM((2 * DA_HEADS, 1), F32), pltpu.VMEM((2 * DA_HEADS, 1), F32),
                            pltpu.VMEM((2 * DA_HEADS, HEAD_W), F32)],
        ),
        out_shape=jax.ShapeDtypeStruct((bs, DA_HEADS, HEAD_W), F32),
        compiler_params=_params("parallel", "arbitrary"),
        name="decode_attn",
    )(page_table.reshape(-1), lam_p, q.reshape(bs, DA_HEADS, HEAD_W), k_new.reshape(bs, DA_HEADS, HEAD_W),
      v_new.reshape(bs, DA_HEADS, HEAD_W), cache_k, cache_v, subln)


def _layernorm(x, g):
    xc = x - jnp.mean(x, axis=-1, keepdims=True)
    return xc * lax.rsqrt(jnp.mean(xc * xc, axis=-1, keepdims=True) + EPS) * g


def _gm_kernel(main_ref, gn_ref, ws_ref, bt_ref, y_ref, *, n_chunks):
    q = ws_ref.shape[1]
    r = lax.broadcasted_iota(jnp.int32, (q, q), 0)
    c = lax.broadcasted_iota(jnp.int32, (q, q), 1)
    wm = [jnp.where(c <= r, ws_ref[g], 0.0).astype(BF16) for g in range(GM_GROUPS)]
    for ch in range(n_chunks):
        rows = slice(ch * q, (ch + 1) * q)
        u = jax.nn.gelu(main_ref[rows, 0:MIX_W])
        vn = _layernorm(jax.nn.gelu(main_ref[rows, MIX_W:2 * MIX_W]), gn_ref[...]).astype(BF16)
        for g in range(GM_GROUPS):
            sl = slice(g * LANES, (g + 1) * LANES)
            sv = _dot(wm[g], vn[:, sl]) + bt_ref[:, g:g + 1]
            y_ref[rows, sl] = u[:, sl] * sv


def _gm_mixer(main, gn, ws, bias, tm):
    t = main.shape[0]
    q = ws.shape[1]
    bias_t = bias.T
    fixed2 = lambda i: (0, 0)
    return pl.pallas_call(
        functools.partial(_gm_kernel, n_chunks=tm // q),
        grid=(t // tm,),
        in_specs=[pl.BlockSpec((tm, 2 * MIX_W), lambda i: (i, 0)), pl.BlockSpec((1, MIX_W), fixed2),
                  pl.BlockSpec(ws.shape, lambda i: (0, 0, 0)), pl.BlockSpec(bias_t.shape, fixed2)],
        out_specs=pl.BlockSpec((tm, MIX_W), lambda i: (i, 0)),
        out_shape=jax.ShapeDtypeStruct((t, MIX_W), F32),
        compiler_params=_params("parallel"),
        name="gm_mixer",
    )(main, gn, ws, bias_t)


def _gm_single_kernel(main_ref, gn_ref, w0_ref, b0_ref, y_ref, vn_ref):
    u = jax.nn.gelu(main_ref[:, 0:MIX_W])
    vn = _layernorm(jax.nn.gelu(main_ref[:, MIX_W:2 * MIX_W]), gn_ref[...])
    vn_ref[...] = vn
    y_ref[...] = u * (w0_ref[...] * vn + b0_ref[...])


def _gm_single(main, gn, ws, bias):
    t = main.shape[0]
    gd = MIX_W // GM_GROUPS
    w0 = jnp.repeat(ws[:, 0, 0], gd)[None, :]
    b0 = jnp.repeat(bias[:, 0], gd)[None, :]
    full = lambda shape: pl.BlockSpec(shape, lambda: (0,) * len(shape))
    return pl.pallas_call(
        _gm_single_kernel,
        in_specs=[full(main.shape), full(gn.shape), full(w0.shape), full(b0.shape)],
        out_specs=[full((t, MIX_W)), full((t, MIX_W))],
        out_shape=[jax.ShapeDtypeStruct((t, MIX_W), F32), jax.ShapeDtypeStruct((t, MIX_W), F32)],
        name="gm_single",
    )(main, gn, w0, b0)


def _pair_lanes(col_lo, col_hi, lo_mask):
    return jnp.where(lo_mask, col_lo, col_hi)


def _group_norm_store(y_ref, rows, g, ys, ng_ref):
    gw = MIX_W // SSD_GROUPS
    yg = jnp.concatenate(ys, axis=-1)
    y_ref[rows, g * gw:(g + 1) * gw] = _rms_scale(yg, ng_ref[:, g * gw:(g + 1) * gw])


def _ssd_kernel(xbc_ref, dt_ref, z_ref, cw_ref, cb_ref, dtb_ref, alog_ref, dsk_ref, ng_ref,
                y_ref, st_ref, xfull_sc):
    q = xbc_ref.shape[0]
    pad = 8

    @pl.when(pl.program_id(1) == 0)
    def _():
        st_ref[...] = jnp.zeros(st_ref.shape, F32)
        xfull_sc[0:pad, :] = jnp.zeros((pad, SSD_CONV_DIM), F32)

    x = xbc_ref[...]
    xfull_sc[pad:pad + q, :] = x
    conv = cb_ref[...] + cw_ref[SSD_CONV - 1:SSD_CONV, :] * x
    for tap in range(SSD_CONV - 1):
        back = SSD_CONV - 1 - tap
        conv = conv + cw_ref[tap:tap + 1, :] * xfull_sc[pad - back:pad - back + q, :]
    xfull_sc[0:pad, :] = x[q - pad:q, :]
    act = jax.nn.silu(conv)

    dtv = jax.nn.softplus(dt_ref[...] + dtb_ref[...])
    d_a = dtv * (-jnp.exp(alog_ref[...]))
    r = lax.broadcasted_iota(jnp.int32, (q, q), 0)
    c = lax.broadcasted_iota(jnp.int32, (q, q), 1)
    tril = c <= r
    cum = jnp.dot(tril.astype(F32), d_a, preferred_element_type=F32, precision=lax.Precision.HIGHEST)
    cum_t = cum.T
    dt_t = dtv.T
    cum_last = cum[q - 1:q, :]
    w_end = jnp.exp(cum_last - cum) * dtv
    e_cum = jnp.exp(cum)
    c_dec = jnp.exp(cum_last)

    lo = lax.broadcasted_iota(jnp.int32, (q, LANES), 1) < SSD_HD
    top = lax.broadcasted_iota(jnp.int32, (LANES, SSD_N), 0) < SSD_HD
    rows = slice(0, q)
    for g in range(SSD_GROUPS):
        bsl = slice(MIX_W + g * SSD_N, MIX_W + (g + 1) * SSD_N)
        csl = slice(MIX_W + SSD_GROUPS * SSD_N + g * SSD_N, MIX_W + SSD_GROUPS * SSD_N + (g + 1) * SSD_N)
        bcg = act[:, bsl].astype(BF16)
        ccg = act[:, csl].astype(BF16)
        cb = _dot_nt(ccg, bcg)
        ys = []
        for pp in range(SSD_PAIRS // SSD_GROUPS):
            p = g * (SSD_PAIRS // SSD_GROUPS) + pp
            h0, h1 = 2 * p, 2 * p + 1
            sl = slice(p * LANES, (p + 1) * LANES)
            xp = act[:, sl]
            xpb = xp.astype(BF16)
            zero = jnp.zeros_like(xpb)
            yd = None
            for h, xm in ((h0, jnp.where(lo, xpb, zero)), (h1, jnp.where(lo, zero, xpb))):
                seg = cum[:, h:h + 1] - cum_t[h:h + 1, :]
                dec = jnp.exp(jnp.where(tril, seg, -jnp.inf))
                w = (cb * dec * dt_t[h:h + 1, :]).astype(BF16)
                part = _dot(w, xm)
                yd = part if yd is None else yd + part
            state = st_ref[0, p]
            yo = _dot_nt(ccg, state.astype(BF16)) * _pair_lanes(e_cum[:, h0:h0 + 1], e_cum[:, h1:h1 + 1], lo)
            xw = xp * _pair_lanes(w_end[:, h0:h0 + 1], w_end[:, h1:h1 + 1], lo)
            upd = _dot(xw.T.astype(BF16), bcg)
            dec_rows = jnp.where(top, jnp.broadcast_to(c_dec[:, h0:h0 + 1], top.shape),
                                 jnp.broadcast_to(c_dec[:, h1:h1 + 1], top.shape))
            st_ref[0, p] = state * dec_rows + upd
            y = yd + yo + dsk_ref[:, sl] * xp
            ys.append(y * jax.nn.silu(z_ref[:, sl]))
        _group_norm_store(y_ref, rows, g, ys, ng_ref)


def _ssd_mixer(xbc, dt, z, batch, cw, cb, dtb, alog, dsk, ng):
    t = xbc.shape[0]
    q = min(SSD_CHUNK, t // batch)
    nc = t // batch // q
    tok = lambda b, c: (b * nc + c, 0)
    fixed = lambda b, c: (0, 0)
    par = lambda a: pl.BlockSpec(a.shape, fixed)
    return pl.pallas_call(
        _ssd_kernel,
        grid=(batch, nc),
        in_specs=[pl.BlockSpec((q, SSD_CONV_DIM), tok), pl.BlockSpec((q, LANES), tok), pl.BlockSpec((q, MIX_W), tok),
                  par(cw), par(cb), par(dtb), par(alog), par(dsk), par(ng)],
        out_specs=[pl.BlockSpec((q, MIX_W), tok),
                   pl.BlockSpec((1, SSD_PAIRS, LANES, SSD_N), lambda b, c: (b, 0, 0, 0))],
        out_shape=[jax.ShapeDtypeStruct((t, MIX_W), F32),
                   jax.ShapeDtypeStruct((batch, SSD_PAIRS, LANES, SSD_N), F32)],
        scratch_shapes=[pltpu.VMEM((8 + q, SSD_CONV_DIM), F32)],
        compiler_params=_params("parallel", "arbitrary"),
        name="ssd_mixer",
    )(xbc, dt, z, cw, cb, dtb, alog, dsk, ng)


def _ssd_single_kernel(xbc_ref, dt_ref, z_ref, cs_ref, st_ref, cw_ref, cb_ref, dtb_ref, alog_ref, dsk_ref, ng_ref,
                       y_ref, cn_ref, sn_ref):
    x = xbc_ref[0]
    buf = cs_ref[0]
    conv = cb_ref[...] + cw_ref[SSD_CONV - 1:SSD_CONV, :] * x
    for tap in range(SSD_CONV - 1):
        conv = conv + cw_ref[tap:tap + 1, :] * buf[tap:tap + 1, :]
    cn_ref[0, 0:SSD_CONV - 2, :] = buf[1:SSD_CONV - 1, :]
    cn_ref[0, SSD_CONV - 2:SSD_CONV - 1, :] = x
    act = jax.nn.silu(conv)

    dtv = jax.nn.softplus(dt_ref[0] + dtb_ref[...])
    e_da = jnp.exp(dtv * (-jnp.exp(alog_ref[...])))
    lo = lax.broadcasted_iota(jnp.int32, (1, LANES), 1) < SSD_HD
    top = lax.broadcasted_iota(jnp.int32, (LANES, SSD_N), 0) < SSD_HD
    z = z_ref[0]
    rows = slice(0, 1)
    for g in range(SSD_GROUPS):
        bsl = slice(MIX_W + g * SSD_N, MIX_W + (g + 1) * SSD_N)
        csl = slice(MIX_W + SSD_GROUPS * SSD_N + g * SSD_N, MIX_W + SSD_GROUPS * SSD_N + (g + 1) * SSD_N)
        bcg = act[:, bsl]
        ccg = act[:, csl]
        cbg = jnp.sum(ccg.astype(BF16).astype(F32) * bcg.astype(BF16).astype(F32), axis=-1, keepdims=True)
        cc8 = jnp.broadcast_to(ccg, (8, SSD_N)).astype(BF16)
        ys = []
        for pp in range(SSD_PAIRS // SSD_GROUPS):
            p = g * (SSD_PAIRS // SSD_GROUPS) + pp
            h0, h1 = 2 * p, 2 * p + 1
            sl = slice(p * LANES, (p + 1) * LANES)
            xp = act[:, sl]
            dtp = _pair_lanes(dtv[:, h0:h0 + 1], dtv[:, h1:h1 + 1], lo)
            ep = _pair_lanes(e_da[:, h0:h0 + 1], e_da[:, h1:h1 + 1], lo)
            state = st_ref[0, p]
            yo = _dot_nt(cc8, state.astype(BF16))[0:1, :] * ep
            xw = xp * dtp
            x_rows = jnp.broadcast_to(xw, (LANES, LANES)).T
            dec_rows = jnp.where(top, jnp.broadcast_to(e_da[:, h0:h0 + 1], top.shape),
                                 jnp.broadcast_to(e_da[:, h1:h1 + 1], top.shape))
            sn_ref[0, p] = state * dec_rows + x_rows * bcg
            y = cbg * xw + yo + dsk_ref[:, sl] * xp
            ys.append(y * jax.nn.silu(z[:, sl]))
        _group_norm_store(y_ref.at[0], rows, g, ys, ng_ref)


def _ssd_single(xbc, dt, z, conv_state, ssm_state, cw, cb, dtb, alog, dsk, ng):
    bs = xbc.shape[0]
    seq3 = lambda width: pl.BlockSpec((1, 1, width), lambda b: (b, 0, 0))
    par = lambda a: pl.BlockSpec(a.shape, lambda b: (0, 0))
    conv_spec = pl.BlockSpec((1, SSD_CONV - 1, SSD_CONV_DIM), lambda b: (b, 0, 0))
    st_spec = pl.BlockSpec((1, SSD_PAIRS, LANES, SSD_N), lambda b: (b, 0, 0, 0))
    return pl.pallas_call(
        _ssd_single_kernel,
        grid=(bs,),
        in_specs=[seq3(SSD_CONV_DIM), seq3(LANES), seq3(MIX_W), conv_spec, st_spec,
                  par(cw), par(cb), par(dtb), par(alog), par(dsk), par(ng)],
        out_specs=[seq3(MIX_W), conv_spec, st_spec],
        out_shape=[jax.ShapeDtypeStruct((bs, 1, MIX_W), F32),
                   jax.ShapeDtypeStruct(conv_state.shape, F32),
                   jax.ShapeDtypeStruct(ssm_state.shape, F32)],
        compiler_params=_params("parallel"),
        name="ssd_single",
    )(xbc[:, None, :], dt[:, None, :], z[:, None, :], conv_state, ssm_state, cw, cb, dtb, alog, dsk, ng)


def _softmax_rows(s):
    e = jnp.exp(s - jnp.max(s, axis=-1, keepdims=True))
    return e / jnp.sum(e, axis=-1, keepdims=True)


def _out_proj_residual(h, mo, ma, wo_ref, g_ref):
    out = _dot(mo.astype(BF16), wo_ref[0:MIX_W, :]) + _dot(ma.astype(BF16), wo_ref[MIX_W:MIX_W + MEM_W, :])
    return h + _rms_scale(out, g_ref[...])


def _finish_kernel(*refs, gated):
    if gated:
        h_ref, a_ref, gate_ref, mq_ref, mg_ref, mk_ref, mv_ref, wo_ref, g_ref, o_ref = refs
        mo = jax.nn.silu(gate_ref[...]) * a_ref[...]
    else:
        h_ref, a_ref, mq_ref, mg_ref, mk_ref, mv_ref, wo_ref, g_ref, o_ref = refs
        mo = a_ref[...]
    heads = []
    for hd in range(MEM_HEADS):
        sl = slice(hd * MEM_HD, (hd + 1) * MEM_HD)
        p = _softmax_rows(_dot_nt(mq_ref[:, sl], mk_ref[0, :, sl]) * (MEM_HD ** -0.5))
        heads.append(_dot(p.astype(BF16), mv_ref[0, :, sl]))
    ma = jax.nn.silu(mg_ref[...]) * jnp.concatenate(heads, axis=-1)
    o_ref[...] = _out_proj_residual(h_ref[...], mo, ma, wo_ref, g_ref)


def _finish_prompt(h, a, gate, mq, mg, mkb, mvb, layer, batch, wo, g_post, tm):
    t = h.shape[0]
    n_mem = mkb.shape[1]
    blocks_per_batch = t // batch // tm
    row = lambda i: (i, 0)
    fixed = lambda i: (0, 0)
    wide = pl.BlockSpec((tm, MIX_W), row)
    memw = pl.BlockSpec((tm, MEM_W), row)
    memkv = pl.BlockSpec((1, n_mem, MEM_W), lambda i: (layer * batch + i // blocks_per_batch, 0, 0))
    gated = gate is not None
    ins = [h, a] + ([gate] if gated else []) + [mq, mg, mkb, mvb, wo, g_post]
    specs = [wide, wide] + ([wide] if gated else []) + [memw, memw, memkv, memkv,
                                                         pl.BlockSpec(wo.shape, fixed), pl.BlockSpec((1, D_MODEL), fixed)]
    return pl.pallas_call(
        functools.partial(_finish_kernel, gated=gated),
        grid=(t // tm,),
        in_specs=specs,
        out_specs=wide,
        out_shape=jax.ShapeDtypeStruct((t, D_MODEL), F32),
        compiler_params=_params("parallel"),
        name="finish_prompt",
    )(*ins)


def _mem_decode_kernel(q_ref, mk_ref, mv_ref, o_ref):
    rows = 8
    q = jnp.broadcast_to(q_ref[0], (rows, MEM_W))
    lane_head = lax.broadcasted_iota(jnp.int32, (rows, MEM_W), 1) // MEM_HD
    row = lax.broadcasted_iota(jnp.int32, (rows, MEM_W), 0)
    own = lane_head == row
    qbd = jnp.where(own, q, 0.0).astype(BF16)
    p = _softmax_rows(_dot_nt(qbd, mk_ref[0].astype(BF16)) * (MEM_HD ** -0.5))
    o = _dot(p.astype(BF16), mv_ref[0].astype(BF16))
    o_ref[0] = jnp.sum(jnp.where(own, o, 0.0), axis=0, keepdims=True)


def _mem_decode(mq, cache_mk, cache_mv, layer):
    bs = mq.shape[0]
    n_mem = cache_mk.shape[1]
    seq = pl.BlockSpec((1, 1, MEM_W), lambda b: (b, 0, 0))
    kv = pl.BlockSpec((1, n_mem, MEM_W), lambda b: (b, 0, layer))
    out = pl.pallas_call(
        _mem_decode_kernel,
        grid=(bs,),
        in_specs=[seq, kv, kv],
        out_specs=seq,
        out_shape=jax.ShapeDtypeStruct((bs, 1, MEM_W), F32),
        compiler_params=_params("parallel"),
        name="mem_decode",
    )(mq[:, None, :], cache_mk, cache_mv)
    return out[:, 0, :]


def _finish_single_kernel(*refs, gated):
    if gated:
        h_ref, a_ref, gate_ref, matt_ref, mg_ref, wo_ref, g_ref, o_ref = refs
        mo = jax.nn.silu(gate_ref[...]) * a_ref[...]
    else:
        h_ref, a_ref, matt_ref, mg_ref, wo_ref, g_ref, o_ref = refs
        mo = a_ref[...]
    ma = jax.nn.silu(mg_ref[...]) * matt_ref[...]
    o_ref[...] = _out_proj_residual(h_ref[...], mo, ma, wo_ref, g_ref)


def _finish_single(h, a, gate, matt, mg, wo, g_post):
    gated = gate is not None
    ins = [h, a] + ([gate] if gated else []) + [matt, mg, wo, g_post]
    full = lambda arr: pl.BlockSpec(arr.shape, lambda: (0,) * arr.ndim)
    return pl.pallas_call(
        functools.partial(_finish_single_kernel, gated=gated),
        in_specs=[full(x) for x in ins],
        out_specs=pl.BlockSpec(h.shape, lambda: (0, 0)),
        out_shape=jax.ShapeDtypeStruct(h.shape, F32),
        compiler_params=pltpu.CompilerParams(vmem_limit_bytes=VMEM_LIMIT),
        name="finish_single",
    )(*ins)


def _rope_tables(pos):
    inv = ROPE_THETA ** (-jnp.arange(ROT_HALF, dtype=F32) / ROT_HALF)
    ang = pos.astype(F32)[:, None] * inv[None, :]
    cos, sin = jnp.cos(ang), jnp.sin(ang)
    n = pos.shape[0]
    rest = DA_HD - ROT_DIM
    c64 = jnp.concatenate([cos, cos, jnp.ones((n, rest), F32)], axis=1)
    s64 = jnp.concatenate([-sin, sin, jnp.zeros((n, rest), F32)], axis=1)
    return jnp.tile(c64, (1, 2)), jnp.tile(s64, (1, 2))


def _pad_lanes(v):
    return jnp.pad(v, (0, LANES - v.shape[0]))[None, :]


def kernel(x_prompt, x_sample, cache_attn_k, cache_attn_v, cache_mem_k, cache_mem_v, state_ssm, state_conv, page_table, mem_prompt, norm_pre, norm_post, w_mem_kv, w_out, attn_w_in, attn_lambda, attn_subln, gm_w_in, gm_norm, gm_ws, gm_bias, ssd_w_in, ssd_conv_w, ssd_conv_b, ssd_dt_bias, ssd_a_log, ssd_d, ssd_norm):
    bp, sp, _ = x_prompt.shape
    bs, ts, _ = x_sample.shape
    assert ts == 1
    depth = norm_pre.shape[0]
    n_mem = mem_prompt.shape[1]
    page = cache_attn_k.shape[1]
    past = page_table.shape[1] * page
    tp = bp * sp
    tm = min(256, sp)
    blk = min(256, sp)

    cos_p, sin_p = _rope_tables(jnp.arange(sp, dtype=jnp.int32))
    cos_s, sin_s = _rope_tables(jnp.full((bs,), past, jnp.int32))

    mk_all, mv_all, mkb, mvb = _memkv(mem_prompt.reshape(bp * n_mem, D_MODEL), w_mem_kv)
    mkb = mkb.reshape(depth * bp, n_mem, MEM_W)
    mvb = mvb.reshape(depth * bp, n_mem, MEM_W)
    cache_mk = cache_mem_k.reshape(bs, n_mem, depth * MEM_W)
    cache_mv = cache_mem_v.reshape(bs, n_mem, depth * MEM_W)

    hp = x_prompt.reshape(tp, D_MODEL)
    hs = x_sample.reshape(bs, D_MODEL)
    k_p, v_p, k_s, v_s = [], [], [], []
    ssm_p, conv_p, ssm_s, conv_s, gm_v_s = [], [], [], [], []
    for i in range(depth):
        kind, j = i % 3, i // 3
        g_pre = norm_pre[i][None, :]
        g_post = norm_post[i][None, :]
        wo = w_out[i].astype(BF16)
        if kind == 0:
            lam_init = 0.8 - 0.6 * math.exp(-0.3 * i)
            w = attn_w_in[j].astype(BF16)
            subln = attn_subln[j][None, :]
            q, k, v, kb, vb, gate_p, mq_p, mg_p = _inproj_attn(hp, g_pre, w, cos_p, sin_p, tm, BF16, BF16)
            a_p = _flash(attn_lambda[j], q.reshape(bp, sp, MIX_W), kb.reshape(bp, sp, MIX_W),
                         vb.reshape(bp, sp, MIX_W), subln, blk, lam_init).reshape(tp, MIX_W)
            k_p.append(k.reshape(bp, sp, DA_HEADS, HEAD_W))
            v_p.append(v.reshape(bp, sp, DA_HEADS, HEAD_W))
            q, k, v, _, _, gate_s, mq_s, mg_s = _inproj_attn(hs, g_pre, w, cos_s, sin_s, bs, F32, F32)
            a_s = _decode_attn(page_table, attn_lambda[j], q, k, v, cache_attn_k, cache_attn_v, subln, j,
                               lam_init).reshape(bs, MIX_W)
            k_s.append(k.reshape(bs, ts, DA_HEADS, HEAD_W))
            v_s.append(v.reshape(bs, ts, DA_HEADS, HEAD_W))
        elif kind == 1:
            w = gm_w_in[j].astype(BF16)
            gn = gm_norm[j][None, :]
            segs = lambda mq_dt: ((2 * MIX_W, F32), (MIX_W, F32), (MEM_W, mq_dt), (MEM_W, F32))
            main, gate_p, mq_p, mg_p = _inproj_plain(hp, g_pre, w, tm, segs(BF16), "inproj_gm")
            a_p = _gm_mixer(main, gn, gm_ws[j][:, :min(GM_CHUNK, sp), :min(GM_CHUNK, sp)],
                            gm_bias[j][:, :min(GM_CHUNK, sp)], tm)
            main, gate_s, mq_s, mg_s = _inproj_plain(hs, g_pre, w, bs, segs(F32), "inproj_gm")
            a_s, vn_s = _gm_single(main, gn, gm_ws[j], gm_bias[j])
            gm_v_s.append(vn_s.reshape(bs, ts, MIX_W))
        else:
            wf = ssd_w_in[j]
            n_dt = SSD_HEADS
            tail = SSD_CONV_DIM + n_dt
            w = jnp.concatenate([wf[:, :SSD_CONV_DIM], wf[:, tail:], wf[:, SSD_CONV_DIM:tail],
                                 jnp.zeros((D_MODEL, LANES - n_dt), F32)], axis=1).astype(BF16)
            segs = lambda mq_dt: ((SSD_CONV_DIM, F32), (MIX_W, F32), (MEM_W, mq_dt), (MEM_W, F32), (LANES, F32))
            pars = (ssd_conv_w[j], ssd_conv_b[j][None, :], _pad_lanes(ssd_dt_bias[j]), _pad_lanes(ssd_a_log[j]),
                    jnp.repeat(ssd_d[j], SSD_HD)[None, :], ssd_norm[j][None, :])
            xbc, z, mq_p, mg_p, dt = _inproj_plain(hp, g_pre, w, tm, segs(BF16), "inproj_ssd")
            a_p, st = _ssd_mixer(xbc, dt, z, bp, *pars)
            gate_p = None
            ssm_p.append(st.reshape(bp, SSD_HEADS, SSD_HD, SSD_N))
            conv_p.append(xbc.reshape(bp, sp, SSD_CONV_DIM)[:, sp - (SSD_CONV - 1):, :])
            xbc, z, mq_s, mg_s, dt = _inproj_plain(hs, g_pre, w, bs, segs(F32), "inproj_ssd")
            a_s, cn, sn = _ssd_single(xbc, dt, z, state_conv[j],
                                      state_ssm[j].reshape(bs, SSD_PAIRS, LANES, SSD_N), *pars)
            a_s = a_s.reshape(bs, MIX_W)
            gate_s = None
            ssm_s.append(sn.reshape(bs, SSD_HEADS, SSD_HD, SSD_N))
            conv_s.append(cn)
        hp = _finish_prompt(hp, a_p, gate_p, mq_p, mg_p, mkb, mvb, i, bp, wo, g_post, tm)
        matt = _mem_decode(mq_s, cache_mk, cache_mv, i)
        hs = _finish_single(hs, a_s, gate_s, matt, mg_s, wo, g_post)

    return (hp.reshape(bp, sp, D_MODEL), hs.reshape(bs, ts, D_MODEL),
            jnp.stack(k_p, axis=2), jnp.stack(v_p, axis=2),
            mk_all.reshape(bp, n_mem, depth, MEM_HEADS, MEM_HD), mv_all.reshape(bp, n_mem, depth, MEM_HEADS, MEM_HD),
            jnp.stack(ssm_p, axis=0), jnp.stack(conv_p, axis=0),
            jnp.stack(k_s, axis=2), jnp.stack(v_s, axis=2),
            jnp.stack(ssm_s, axis=0), jnp.stack(conv_s, axis=0),
            jnp.stack(gm_v_s, axis=2))
```

```python
import functools
import math

import jax
import jax.numpy as jnp
from jax import lax
from jax.experimental import pallas as pl
from jax.experimental.pallas import tpu as pltpu

F32 = jnp.float32
BF16 = jnp.bfloat16

D_MODEL = 1024
MIX_W = D_MODEL
MEM_HEADS = 4
MEM_HD = 128
MEM_W = MEM_HEADS * MEM_HD
DA_HEADS = 8
DA_HD = 64
HEAD_W = 2 * DA_HD
ROT_DIM = DA_HD // 4
ROT_HALF = ROT_DIM // 2
ROPE_THETA = 500000.0
GM_CHUNK = 128
GM_GROUPS = 8
SSD_HD = 64
SSD_HEADS = MIX_W // SSD_HD
SSD_GROUPS = 4
SSD_N = 128
SSD_CONV = 4
SSD_CHUNK = 128
SSD_PAIRS = SSD_HEADS // 2
SSD_CONV_DIM = MIX_W + 2 * SSD_GROUPS * SSD_N
EPS = 1e-6
NEG_BIG = -1e30
LANES = 128
VMEM_LIMIT = 56 * 1024 * 1024
FLASH_TQ = 512
FLASH_ROWS = 32
DECODE_PAGES = 8
Q_SCALE = DA_HD ** -0.5 * math.log2(math.e)
COL_CHAINS = 1

_NT = (((1,), (1,)), ((), ()))


def _params(*sem):
    return pltpu.CompilerParams(dimension_semantics=sem, vmem_limit_bytes=VMEM_LIMIT)


def _dot(a, b):
    return jnp.dot(a, b, preferred_element_type=F32)


def _dot_nt(a, b):
    return lax.dot_general(a, b, _NT, preferred_element_type=F32)


def _rms_scale(x, g):
    return x * lax.rsqrt(jnp.mean(x * x, axis=-1, keepdims=True) + EPS) * g


def _rope_head(x, c, s):
    lane = lax.broadcasted_iota(jnp.int32, x.shape, 1) & (DA_HD - 1)
    partner = jnp.where(lane < ROT_HALF, pltpu.roll(x, LANES - ROT_HALF, 1), pltpu.roll(x, ROT_HALF, 1))
    return x * c + partner * s


def _inproj_attn_kernel(x_ref, g_ref, w_ref, c_ref, s_ref, *refs):
    q_ref, k_ref, v_ref, kb_ref, vb_ref, gate_ref, mq_ref, mg_ref = refs[-8:]
    xn = _rms_scale(x_ref[...], g_ref[...]).astype(BF16)
    c = c_ref[...]
    s = s_ref[...]
    q = _dot(xn, w_ref[:, 0:MIX_W])
    k = _dot(xn, w_ref[:, MIX_W:2 * MIX_W])
    for h in range(DA_HEADS):
        sl = slice(h * HEAD_W, (h + 1) * HEAD_W)
        q_ref[:, sl] = (_rope_head(q[:, sl], c, s) * Q_SCALE).astype(q_ref.dtype)
        kr = _rope_head(k[:, sl], c, s)
        k_ref[:, sl] = kr
        kb_ref[:, sl] = kr.astype(BF16)
    v = _dot(xn, w_ref[:, 2 * MIX_W:3 * MIX_W])
    v_ref[...] = v
    vb_ref[...] = v.astype(BF16)
    o = 3 * MIX_W
    gate_ref[...] = _dot(xn, w_ref[:, o:o + MIX_W])
    mq_ref[...] = _dot(xn, w_ref[:, o + MIX_W:o + MIX_W + MEM_W]).astype(mq_ref.dtype)
    mg_ref[...] = _dot(xn, w_ref[:, o + MIX_W + MEM_W:o + MIX_W + 2 * MEM_W])


def _inproj_attn(x, g, w, cos_t, sin_t, tm, q_dtype, mq_dtype, slot, n_slots, kv_all):
    t = x.shape[0]
    n_pos_blocks = cos_t.shape[0] // tm
    row = lambda i: (i, 0)
    fixed = lambda i: (0, 0)
    pos = lambda i: (i % n_pos_blocks, 0)
    wide = pl.BlockSpec((tm, MIX_W), row)
    mem = pl.BlockSpec((tm, MEM_W), row)
    kv_out = pl.BlockSpec((tm, MIX_W), lambda i: (i, slot))
    kv_shape = jax.ShapeDtypeStruct((t, n_slots * MIX_W), F32)
    ins = [x, g, w, cos_t, sin_t]
    in_specs = [wide, pl.BlockSpec((1, D_MODEL), fixed), pl.BlockSpec(w.shape, fixed),
                pl.BlockSpec((tm, HEAD_W), pos), pl.BlockSpec((tm, HEAD_W), pos)]
    aliases = {}
    if kv_all is not None:
        aliases = {len(ins): 1, len(ins) + 1: 2}
        ins += list(kv_all)
        in_specs += [pl.BlockSpec(memory_space=pl.ANY)] * 2
    return pl.pallas_call(
        _inproj_attn_kernel,
        grid=(t // tm,),
        in_specs=in_specs,
        out_specs=[wide, kv_out, kv_out, wide, wide, wide, mem, mem],
        out_shape=[jax.ShapeDtypeStruct((t, MIX_W), q_dtype), kv_shape, kv_shape,
                   jax.ShapeDtypeStruct((t, MIX_W), BF16), jax.ShapeDtypeStruct((t, MIX_W), BF16),
                   jax.ShapeDtypeStruct((t, MIX_W), F32),
                   jax.ShapeDtypeStruct((t, MEM_W), mq_dtype), jax.ShapeDtypeStruct((t, MEM_W), F32)],
        input_output_aliases=aliases,
        compiler_params=_params("parallel"),
        name="inproj_attn",
    )(*ins)


def _inproj_plain_kernel(x_ref, g_ref, w_ref, *out_refs, starts):
    xn = _rms_scale(x_ref[...], g_ref[...]).astype(BF16)
    for ref, start in zip(out_refs, starts):
        ref[...] = _dot(xn, w_ref[:, start:start + ref.shape[1]]).astype(ref.dtype)


def _inproj_plain(x, g, w, tm, segments, name):
    t = x.shape[0]
    starts, acc = [], 0
    for width, _ in segments:
        starts.append(acc)
        acc += width
    assert acc == w.shape[1]
    row = lambda i: (i, 0)
    fixed = lambda i: (0, 0)
    return pl.pallas_call(
        functools.partial(_inproj_plain_kernel, starts=tuple(starts)),
        grid=(t // tm,),
        in_specs=[pl.BlockSpec((tm, D_MODEL), row), pl.BlockSpec((1, D_MODEL), fixed),
                  pl.BlockSpec(w.shape, fixed)],
        out_specs=[pl.BlockSpec((tm, width), row) for width, _ in segments],
        out_shape=[jax.ShapeDtypeStruct((t, width), dt) for width, dt in segments],
        compiler_params=_params("parallel"),
        name=name,
    )(x, g, w)


def _memkv_kernel(x_ref, w_ref, mk_ref, mv_ref, mkb_ref, mvb_ref):
    kv = _dot(x_ref[...].astype(BF16), w_ref[0].astype(BF16))
    k = kv[:, :MEM_W]
    v = kv[:, MEM_W:]
    mk_ref[...] = k
    mv_ref[...] = v
    mkb_ref[0] = k.astype(BF16)
    mvb_ref[0] = v.astype(BF16)


def _memkv(mem_tokens, w_mem_kv):
    tm = mem_tokens.shape[0]
    depth = w_mem_kv.shape[0]
    f32_out = pl.BlockSpec((tm, MEM_W), lambda i: (0, i))
    bf_out = pl.BlockSpec((1, tm, MEM_W), lambda i: (i, 0, 0))
    return pl.pallas_call(
        _memkv_kernel,
        grid=(depth,),
        in_specs=[pl.BlockSpec((tm, D_MODEL), lambda i: (0, 0)),
                  pl.BlockSpec((1, D_MODEL, 2 * MEM_W), lambda i: (i, 0, 0))],
        out_specs=[f32_out, f32_out, bf_out, bf_out],
        out_shape=[jax.ShapeDtypeStruct((tm, depth * MEM_W), F32), jax.ShapeDtypeStruct((tm, depth * MEM_W), F32),
                   jax.ShapeDtypeStruct((depth, tm, MEM_W), BF16), jax.ShapeDtypeStruct((depth, tm, MEM_W), BF16)],
        compiler_params=_params("parallel"),
        name="memkv",
    )(mem_tokens, w_mem_kv)


def _diff_lambda(lp, lam_init):
    a = jnp.sum(lp[0:1, :] * lp[1:2, :], axis=-1, keepdims=True)
    b = jnp.sum(lp[2:3, :] * lp[3:4, :], axis=-1, keepdims=True)
    return jnp.exp(a) - jnp.exp(b) + lam_init


def _sub_norm(o, g, lam_init):
    return _rms_scale(o, g) * (1.0 - lam_init)


def _col_reduce(x, pair, final):
    del pair
    rows = 8 * COL_CHAINS
    if x.shape[0] % rows == 0 and x.shape[0] > rows:
        x = final(x.reshape(x.shape[0] // rows, rows, x.shape[1]), axis=0)
    return final(x, axis=0, keepdims=True)


def _flash_kernel(lam_ref, q_ref, k_ref, v_ref, g_ref, o_ref,
                  qqt_sc, s_sc, mx_sc, p_sc, al_sc, m_sc, l_sc, acc_sc, *, tq, tk, lam_init):
    assert tq == 2 * tk
    i = pl.program_id(2)
    qt = q_ref[0].astype(F32).T
    feat = lax.broadcasted_iota(jnp.int32, qt.shape, 0)
    qqt_sc[...] = jnp.concatenate([jnp.where(feat < DA_HD, qt, 0.0), jnp.where(feat >= DA_HD, qt, 0.0)],
                                  axis=1).astype(BF16)
    m_sc[...] = jnp.full(m_sc.shape, NEG_BIG, F32)
    l_sc[...] = jnp.zeros(l_sc.shape, F32)
    acc_sc[...] = jnp.zeros(acc_sc.shape, F32)
    p_sc[1] = jnp.zeros(p_sc.shape[1:], BF16)
    al_sc[1] = jnp.ones(al_sc.shape[1:], F32)

    def scores(blk, slot):
        start = pl.multiple_of(blk * tk, tk)
        st = _dot(k_ref[0, pl.ds(start, tk), :], qqt_sc[...])
        s_sc[slot] = st
        mx_sc[slot] = _col_reduce(st, jnp.maximum, jnp.max)

    n = 2 * tq
    chunks = [slice(c * FLASH_ROWS, (c + 1) * FLASH_ROWS) for c in range(tk // FLASH_ROWS)]

    def fold8(x, op):
        return op(x.reshape(x.shape[0] // 8, 8, n), axis=0)

    def probs(slot, diag_blk=None):
        if diag_blk is None:
            mx = mx_sc[slot]
        else:
            part = jnp.full((8, n), NEG_BIG, F32)
            for rows in chunks:
                kpos = diag_blk * tk + rows.start + lax.broadcasted_iota(jnp.int32, (FLASH_ROWS, n), 0)
                qpos = i * tq + (lax.broadcasted_iota(jnp.int32, (FLASH_ROWS, n), 1) & (tq - 1))
                st = jnp.where(kpos <= qpos, s_sc[slot, rows, :], NEG_BIG)
                s_sc[slot, rows, :] = st
                part = jnp.maximum(part, fold8(st, jnp.max))
            mx = jnp.max(part, axis=0, keepdims=True)
        m_old = m_sc[...]
        m_new = jnp.maximum(m_old, mx)
        alpha = jnp.exp2(m_old - m_new)
        part = jnp.zeros((8, n), F32)
        for rows in chunks:
            p = jnp.exp2(s_sc[slot, rows, :] - m_new)
            part = part + fold8(p, jnp.sum)
            p_sc[slot, rows, :] = p.astype(BF16)
        l_sc[...] = alpha * l_sc[...] + jnp.sum(part, axis=0, keepdims=True)
        al_sc[slot] = alpha
        m_sc[...] = m_new

    def values(blk, slot):
        start = pl.multiple_of(blk * tk, tk)
        pv = lax.dot_general(v_ref[0, pl.ds(start, tk), :], p_sc[slot], (((0,), (0,)), ((), ())),
                             preferred_element_type=F32)
        acc_sc[...] = acc_sc[...] * al_sc[slot] + pv

    def pair(t, carry):
        values(jnp.maximum(2 * t - 1, 0), 1)
        scores(2 * t + 1, 1)
        probs(0)
        values(2 * t, 0)
        scores(2 * t + 2, 0)
        probs(1)
        return carry

    scores(0, 0)
    lax.fori_loop(0, i, pair, 0)
    values(jnp.maximum(2 * i - 1, 0), 1)
    scores(2 * i + 1, 1)
    probs(0, 2 * i)
    values(2 * i, 0)
    probs(1, 2 * i + 1)
    values(2 * i + 1, 1)

    o = acc_sc[...] / l_sc[...]
    lam = _diff_lambda(lam_ref[...], lam_init)
    od = o[:, :tq] - lam * o[:, tq:]
    scale = lax.rsqrt(jnp.mean(od * od, axis=0, keepdims=True) + EPS)
    o_ref[0] = (od * scale * (g_ref[...] * (1.0 - lam_init))).T


def _flash(lam_p, q, kb, vb, subln, tq, tk, lam_init):
    b, s, _ = q.shape
    assert tq & (tq - 1) == 0 and tq == 2 * tk and s % tq == 0
    n = 2 * tq
    qo = pl.BlockSpec((1, tq, HEAD_W), lambda bi, h, i: (bi, i, h))
    kv = pl.BlockSpec((1, s, HEAD_W), lambda bi, h, i: (bi, 0, h))
    return pl.pallas_call(
        functools.partial(_flash_kernel, tq=tq, tk=tk, lam_init=lam_init),
        grid=(b, DA_HEADS, s // tq),
        in_specs=[pl.BlockSpec((4, DA_HD), lambda bi, h, i: (0, 0)), qo, kv, kv,
                  pl.BlockSpec((HEAD_W, 1), lambda bi, h, i: (0, 0))],
        out_specs=qo,
        out_shape=jax.ShapeDtypeStruct((b, s, MIX_W), F32),
        scratch_shapes=[pltpu.VMEM((HEAD_W, n), BF16),
                        pltpu.VMEM((2, tk, n), F32), pltpu.VMEM((2, 1, n), F32),
                        pltpu.VMEM((2, tk, n), BF16), pltpu.VMEM((2, 1, n), F32),
                        pltpu.VMEM((1, n), F32), pltpu.VMEM((1, n), F32), pltpu.VMEM((HEAD_W, n), F32)],
        compiler_params=_params("parallel", "parallel", "arbitrary"),
        name="flash_diff",
    )(lam_p, q, kb, vb, subln.reshape(HEAD_W, 1))


def _decode_kernel(pt_ref, lam_ref, q_ref, kn_ref, vn_ref, *refs, n_steps, pps, lam_init):
    del pt_ref
    kc_refs, vc_refs = refs[:pps], refs[pps:2 * pps]
    g_ref, o_ref, m_sc, l_sc, acc_sc = refs[2 * pps:]
    step = pl.program_id(1)
    q8 = q_ref[0]
    lane = lax.broadcasted_iota(jnp.int32, q8.shape, 1)
    zero = jnp.zeros_like(q8)
    qsel = jnp.concatenate([jnp.where(lane < DA_HD, q8, zero), jnp.where(lane >= DA_HD, q8, zero)],
                           axis=0).astype(BF16)

    @pl.when(step == 0)
    def _():
        m_sc[...] = jnp.full(m_sc.shape, NEG_BIG, F32)
        l_sc[...] = jnp.zeros(l_sc.shape, F32)
        acc_sc[...] = jnp.zeros(acc_sc.shape, F32)

    rows = kc_refs[0].shape[1] * DA_HEADS
    col = lax.broadcasted_iota(jnp.int32, (2 * DA_HEADS, rows), 1)
    row = lax.broadcasted_iota(jnp.int32, (2 * DA_HEADS, rows), 0)
    same_head = (col & (DA_HEADS - 1)) == (row & (DA_HEADS - 1))
    scores = [jnp.where(same_head, _dot_nt(qsel, kc[0, :, 0].reshape(rows, HEAD_W).astype(BF16)), NEG_BIG)
              for kc in kc_refs]
    m_old = m_sc[...]
    m_new = m_old
    for s in scores:
        m_new = jnp.maximum(m_new, jnp.max(s, axis=-1, keepdims=True))
    alpha = jnp.exp2(m_old - m_new)
    l_new = alpha * l_sc[...]
    acc_new = alpha * acc_sc[...]
    for s, vc in zip(scores, vc_refs):
        pm = jnp.exp2(s - m_new)
        l_new = l_new + jnp.sum(pm, axis=-1, keepdims=True)
        acc_new = acc_new + _dot(pm.astype(BF16), vc[0, :, 0].reshape(rows, HEAD_W).astype(BF16))
    m_sc[...] = m_new
    l_sc[...] = l_new
    acc_sc[...] = acc_new

    @pl.when(step == n_steps - 1)
    def _():
        kn = kn_ref[0].astype(BF16).astype(F32)
        vn = vn_ref[0].astype(BF16).astype(F32)
        kn2 = jnp.concatenate([kn, kn], axis=0)
        vn2 = jnp.concatenate([vn, vn], axis=0)
        s_n = jnp.sum(qsel.astype(F32) * kn2, axis=-1, keepdims=True)
        m2 = jnp.maximum(m_new, s_n)
        a2 = jnp.exp2(m_new - m2)
        pn = jnp.exp2(s_n - m2)
        l2 = a2 * l_new + pn
        acc2 = a2 * acc_new + pn.astype(BF16).astype(F32) * vn2
        o = acc2 / l2
        lam = _diff_lambda(lam_ref[...], lam_init)
        od = o[:DA_HEADS] - lam * o[DA_HEADS:]
        o_ref[0] = _sub_norm(od, g_ref[...], lam_init)


def _decode_attn(page_table, lam_p, q, k_new, v_new, cache_k, cache_v, subln, layer, lam_init):
    bs, n_pages = page_table.shape
    page = cache_k.shape[1]
    pps = math.gcd(n_pages, DECODE_PAGES)
    per_seq = pl.BlockSpec((1, DA_HEADS, HEAD_W), lambda b, p, pt: (b, 0, 0))
    new_kv = pl.BlockSpec((1, DA_HEADS, HEAD_W), lambda b, p, pt: (b, layer, 0))
    cache = [pl.BlockSpec((1, page, 1, DA_HEADS, HEAD_W),
                          lambda b, p, pt, r=r: (pt[b * n_pages + p * pps + r], 0, layer, 0, 0)) for r in range(pps)]
    return pl.pallas_call(
        functools.partial(_decode_kernel, n_steps=n_pages // pps, pps=pps, lam_init=lam_init),
        grid_spec=pltpu.PrefetchScalarGridSpec(
            num_scalar_prefetch=1,
            grid=(bs, n_pages // pps),
            in_specs=[pl.BlockSpec((4, DA_HD), lambda b, p, pt: (0, 0)), per_seq, new_kv, new_kv, *cache, *cache,
                      pl.BlockSpec((1, HEAD_W), lambda b, p, pt: (0, 0))],
            out_specs=per_seq,
            scratch_shapes=[pltpu.VMEM((2 * DA_HEADS, 1), F32), pltpu.VMEM((2 * DA_HEADS, 1), F32),
                            pltpu.VMEM((2 * DA_HEADS, HEAD_W), F32)],
        ),
        out_shape=jax.ShapeDtypeStruct((bs, DA_HEADS, HEAD_W), F32),
        compiler_params=_params("parallel", "arbitrary"),
        name="decode_attn",
    )(page_table.reshape(-1), lam_p, q.reshape(bs, DA_HEADS, HEAD_W), k_new.reshape(bs, -1, HEAD_W),
      v_new.reshape(bs, -1, HEAD_W), *([cache_k] * pps), *([cache_v] * pps), subln)


def _layernorm(x, g):
    xc = x - jnp.mean(x, axis=-1, keepdims=True)
    return xc * lax.rsqrt(jnp.mean(xc * xc, axis=-1, keepdims=True) + EPS) * g


def _gm_kernel(main_ref, gn_ref, ws_ref, bt_ref, y_ref, *, n_chunks):
    q = ws_ref.shape[1]
    r = lax.broadcasted_iota(jnp.int32, (q, q), 0)
    c = lax.broadcasted_iota(jnp.int32, (q, q), 1)
    wm = [jnp.where(c <= r, ws_ref[g], 0.0).astype(BF16) for g in range(GM_GROUPS)]
    for ch in range(n_chunks):
        rows = slice(ch * q, (ch + 1) * q)
        u = jax.nn.gelu(main_ref[rows, 0:MIX_W])
        vn = _layernorm(jax.nn.gelu(main_ref[rows, MIX_W:2 * MIX_W]), gn_ref[...]).astype(BF16)
        for g in range(GM_GROUPS):
            sl = slice(g * LANES, (g + 1) * LANES)
            sv = _dot(wm[g], vn[:, sl]) + bt_ref[:, g:g + 1]
            y_ref[rows, sl] = u[:, sl] * sv


def _gm_mixer(main, gn, ws, bias, tm):
    t = main.shape[0]
    q = ws.shape[1]
    bias_t = bias.T
    fixed2 = lambda i: (0, 0)
    return pl.pallas_call(
        functools.partial(_gm_kernel, n_chunks=tm // q),
        grid=(t // tm,),
        in_specs=[pl.BlockSpec((tm, 2 * MIX_W), lambda i: (i, 0)), pl.BlockSpec((1, MIX_W), fixed2),
                  pl.BlockSpec(ws.shape, lambda i: (0, 0, 0)), pl.BlockSpec(bias_t.shape, fixed2)],
        out_specs=pl.BlockSpec((tm, MIX_W), lambda i: (i, 0)),
        out_shape=jax.ShapeDtypeStruct((t, MIX_W), F32),
        compiler_params=_params("parallel"),
        name="gm_mixer",
    )(main, gn, ws, bias_t)


def _gm_single_kernel(main_ref, gn_ref, w0_ref, b0_ref, y_ref, vn_ref):
    u = jax.nn.gelu(main_ref[:, 0:MIX_W])
    vn = _layernorm(jax.nn.gelu(main_ref[:, MIX_W:2 * MIX_W]), gn_ref[...])
    vn_ref[...] = vn
    y_ref[...] = u * (w0_ref[...] * vn + b0_ref[...])


def _gm_single(main, gn, ws, bias):
    t = main.shape[0]
    gd = MIX_W // GM_GROUPS
    w0 = jnp.repeat(ws[:, 0, 0], gd)[None, :]
    b0 = jnp.repeat(bias[:, 0], gd)[None, :]
    full = lambda shape: pl.BlockSpec(shape, lambda: (0,) * len(shape))
    return pl.pallas_call(
        _gm_single_kernel,
        in_specs=[full(main.shape), full(gn.shape), full(w0.shape), full(b0.shape)],
        out_specs=[full((t, MIX_W)), full((t, MIX_W))],
        out_shape=[jax.ShapeDtypeStruct((t, MIX_W), F32), jax.ShapeDtypeStruct((t, MIX_W), F32)],
        name="gm_single",
    )(main, gn, w0, b0)


def _pair_lanes(col_lo, col_hi, lo_mask):
    return jnp.where(lo_mask, col_lo, col_hi)


def _group_norm_store(y_ref, rows, g, ys, ng_ref):
    gw = MIX_W // SSD_GROUPS
    yg = jnp.concatenate(ys, axis=-1)
    y_ref[rows, g * gw:(g + 1) * gw] = _rms_scale(yg, ng_ref[:, g * gw:(g + 1) * gw])


def _ssd_kernel(xbc_ref, dt_ref, z_ref, cw_ref, cb_ref, dtb_ref, alog_ref, dsk_ref, ng_ref,
                y_ref, st_ref, xfull_sc):
    q = xbc_ref.shape[0]
    pad = 8

    @pl.when(pl.program_id(1) == 0)
    def _():
        st_ref[...] = jnp.zeros(st_ref.shape, F32)
        xfull_sc[0:pad, :] = jnp.zeros((pad, SSD_CONV_DIM), F32)

    x = xbc_ref[...]
    xfull_sc[pad:pad + q, :] = x
    conv = cb_ref[...] + cw_ref[SSD_CONV - 1:SSD_CONV, :] * x
    for tap in range(SSD_CONV - 1):
        back = SSD_CONV - 1 - tap
        conv = conv + cw_ref[tap:tap + 1, :] * xfull_sc[pad - back:pad - back + q, :]
    xfull_sc[0:pad, :] = x[q - pad:q, :]
    act = jax.nn.silu(conv)

    dtv = jax.nn.softplus(dt_ref[...] + dtb_ref[...])
    d_a = dtv * (-jnp.exp(alog_ref[...]))
    r = lax.broadcasted_iota(jnp.int32, (q, q), 0)
    c = lax.broadcasted_iota(jnp.int32, (q, q), 1)
    tril = c <= r
    cum = jnp.dot(tril.astype(F32), d_a, preferred_element_type=F32, precision=lax.Precision.HIGHEST)
    cum_t = cum.T
    dt_t = dtv.T
    cum_last = cum[q - 1:q, :]
    w_end = jnp.exp(cum_last - cum) * dtv
    e_cum = jnp.exp(cum)
    c_dec = jnp.exp(cum_last)

    lo = lax.broadcasted_iota(jnp.int32, (q, LANES), 1) < SSD_HD
    top = lax.broadcasted_iota(jnp.int32, (LANES, SSD_N), 0) < SSD_HD
    rows = slice(0, q)
    for g in range(SSD_GROUPS):
        bsl = slice(MIX_W + g * SSD_N, MIX_W + (g + 1) * SSD_N)
        csl = slice(MIX_W + SSD_GROUPS * SSD_N + g * SSD_N, MIX_W + SSD_GROUPS * SSD_N + (g + 1) * SSD_N)
        bcg = act[:, bsl].astype(BF16)
        ccg = act[:, csl].astype(BF16)
        cb = _dot_nt(ccg, bcg)
        ys = []
        for pp in range(SSD_PAIRS // SSD_GROUPS):
            p = g * (SSD_PAIRS // SSD_GROUPS) + pp
            h0, h1 = 2 * p, 2 * p + 1
            sl = slice(p * LANES, (p + 1) * LANES)
            xp = act[:, sl]
            xpb = xp.astype(BF16)
            zero = jnp.zeros_like(xpb)
            yd = None
            for h, xm in ((h0, jnp.where(lo, xpb, zero)), (h1, jnp.where(lo, zero, xpb))):
                seg = cum[:, h:h + 1] - cum_t[h:h + 1, :]
                dec = jnp.exp(jnp.where(tril, seg, -jnp.inf))
                w = (cb * dec * dt_t[h:h + 1, :]).astype(BF16)
                part = _dot(w, xm)
                yd = part if yd is None else yd + part
            state = st_ref[0, p]
            yo = _dot_nt(ccg, state.astype(BF16)) * _pair_lanes(e_cum[:, h0:h0 + 1], e_cum[:, h1:h1 + 1], lo)
            xw = xp * _pair_lanes(w_end[:, h0:h0 + 1], w_end[:, h1:h1 + 1], lo)
            upd = _dot(xw.T.astype(BF16), bcg)
            dec_rows = jnp.where(top, jnp.broadcast_to(c_dec[:, h0:h0 + 1], top.shape),
                                 jnp.broadcast_to(c_dec[:, h1:h1 + 1], top.shape))
            st_ref[0, p] = state * dec_rows + upd
            y = yd + yo + dsk_ref[:, sl] * xp
            ys.append(y * jax.nn.silu(z_ref[:, sl]))
        _group_norm_store(y_ref, rows, g, ys, ng_ref)


def _ssd_mixer(xbc, dt, z, batch, cw, cb, dtb, alog, dsk, ng):
    t = xbc.shape[0]
    q = min(SSD_CHUNK, t // batch)
    nc = t // batch // q
    tok = lambda b, c: (b * nc + c, 0)
    fixed = lambda b, c: (0, 0)
    par = lambda a: pl.BlockSpec(a.shape, fixed)
    return pl.pallas_call(
        _ssd_kernel,
        grid=(batch, nc),
        in_specs=[pl.BlockSpec((q, SSD_CONV_DIM), tok), pl.BlockSpec((q, LANES), tok), pl.BlockSpec((q, MIX_W), tok),
                  par(cw), par(cb), par(dtb), par(alog), par(dsk), par(ng)],
        out_specs=[pl.BlockSpec((q, MIX_W), tok),
                   pl.BlockSpec((1, SSD_PAIRS, LANES, SSD_N), lambda b, c: (b, 0, 0, 0))],
        out_shape=[jax.ShapeDtypeStruct((t, MIX_W), F32),
                   jax.ShapeDtypeStruct((batch, SSD_PAIRS, LANES, SSD_N), F32)],
        scratch_shapes=[pltpu.VMEM((8 + q, SSD_CONV_DIM), F32)],
        compiler_params=_params("parallel", "arbitrary"),
        name="ssd_mixer",
    )(xbc, dt, z, cw, cb, dtb, alog, dsk, ng)


def _ssd_single_kernel(xbc_ref, dt_ref, z_ref, cs_ref, st_ref, cw_ref, cb_ref, dtb_ref, alog_ref, dsk_ref, ng_ref,
                       y_ref, cn_ref, sn_ref):
    x = xbc_ref[0]
    buf = cs_ref[0]
    conv = cb_ref[...] + cw_ref[SSD_CONV - 1:SSD_CONV, :] * x
    for tap in range(SSD_CONV - 1):
        conv = conv + cw_ref[tap:tap + 1, :] * buf[tap:tap + 1, :]
    cn_ref[0, 0:SSD_CONV - 2, :] = buf[1:SSD_CONV - 1, :]
    cn_ref[0, SSD_CONV - 2:SSD_CONV - 1, :] = x
    act = jax.nn.silu(conv)

    dtv = jax.nn.softplus(dt_ref[0] + dtb_ref[...])
    e_da = jnp.exp(dtv * (-jnp.exp(alog_ref[...])))
    lo = lax.broadcasted_iota(jnp.int32, (1, LANES), 1) < SSD_HD
    top = lax.broadcasted_iota(jnp.int32, (LANES, SSD_N), 0) < SSD_HD
    z = z_ref[0]
    rows = slice(0, 1)
    for g in range(SSD_GROUPS):
        bsl = slice(MIX_W + g * SSD_N, MIX_W + (g + 1) * SSD_N)
        csl = slice(MIX_W + SSD_GROUPS * SSD_N + g * SSD_N, MIX_W + SSD_GROUPS * SSD_N + (g + 1) * SSD_N)
        bcg = act[:, bsl]
        ccg = act[:, csl]
        cbg = jnp.sum(ccg.astype(BF16).astype(F32) * bcg.astype(BF16).astype(F32), axis=-1, keepdims=True)
        cc8 = jnp.broadcast_to(ccg, (8, SSD_N)).astype(BF16)
        ys = []
        for pp in range(SSD_PAIRS // SSD_GROUPS):
            p = g * (SSD_PAIRS // SSD_GROUPS) + pp
            h0, h1 = 2 * p, 2 * p + 1
            sl = slice(p * LANES, (p + 1) * LANES)
            xp = act[:, sl]
            dtp = _pair_lanes(dtv[:, h0:h0 + 1], dtv[:, h1:h1 + 1], lo)
            ep = _pair_lanes(e_da[:, h0:h0 + 1], e_da[:, h1:h1 + 1], lo)
            state = st_ref[0, p]
            yo = _dot_nt(cc8, state.astype(BF16))[0:1, :] * ep
            xw = xp * dtp
            x_rows = jnp.broadcast_to(xw, (LANES, LANES)).T
            dec_rows = jnp.where(top, jnp.broadcast_to(e_da[:, h0:h0 + 1], top.shape),
                                 jnp.broadcast_to(e_da[:, h1:h1 + 1], top.shape))
            sn_ref[0, p] = state * dec_rows + x_rows * bcg
            y = cbg * xw + yo + dsk_ref[:, sl] * xp
            ys.append(y * jax.nn.silu(z[:, sl]))
        _group_norm_store(y_ref.at[0], rows, g, ys, ng_ref)


def _ssd_single(xbc, dt, z, conv_state, ssm_state, cw, cb, dtb, alog, dsk, ng):
    bs = xbc.shape[0]
    seq3 = lambda width: pl.BlockSpec((1, 1, width), lambda b: (b, 0, 0))
    par = lambda a: pl.BlockSpec(a.shape, lambda b: (0, 0))
    conv_spec = pl.BlockSpec((1, SSD_CONV - 1, SSD_CONV_DIM), lambda b: (b, 0, 0))
    st_spec = pl.BlockSpec((1, SSD_PAIRS, LANES, SSD_N), lambda b: (b, 0, 0, 0))
    return pl.pallas_call(
        _ssd_single_kernel,
        grid=(bs,),
        in_specs=[seq3(SSD_CONV_DIM), seq3(LANES), seq3(MIX_W), conv_spec, st_spec,
                  par(cw), par(cb), par(dtb), par(alog), par(dsk), par(ng)],
        out_specs=[seq3(MIX_W), conv_spec, st_spec],
        out_shape=[jax.ShapeDtypeStruct((bs, 1, MIX_W), F32),
                   jax.ShapeDtypeStruct(conv_state.shape, F32),
                   jax.ShapeDtypeStruct(ssm_state.shape, F32)],
        compiler_params=_params("parallel"),
        name="ssd_single",
    )(xbc[:, None, :], dt[:, None, :], z[:, None, :], conv_state, ssm_state, cw, cb, dtb, alog, dsk, ng)


def _softmax_rows(s):
    e = jnp.exp(s - jnp.max(s, axis=-1, keepdims=True))
    return e / jnp.sum(e, axis=-1, keepdims=True)


def _out_proj_residual(h, mo, ma, wo_ref, g_ref):
    out = _dot(mo.astype(BF16), wo_ref[0:MIX_W, :]) + _dot(ma.astype(BF16), wo_ref[MIX_W:MIX_W + MEM_W, :])
    return h + _rms_scale(out, g_ref[...])


def _finish_kernel(*refs, gated):
    if gated:
        h_ref, a_ref, gate_ref, mq_ref, mg_ref, mk_ref, mv_ref, wo_ref, g_ref, o_ref = refs
        mo = jax.nn.silu(gate_ref[...]) * a_ref[...]
    else:
        h_ref, a_ref, mq_ref, mg_ref, mk_ref, mv_ref, wo_ref, g_ref, o_ref = refs
        mo = a_ref[...]
    heads = []
    for hd in range(MEM_HEADS):
        sl = slice(hd * MEM_HD, (hd + 1) * MEM_HD)
        p = _softmax_rows(_dot_nt(mq_ref[:, sl], mk_ref[0, :, sl]) * (MEM_HD ** -0.5))
        heads.append(_dot(p.astype(BF16), mv_ref[0, :, sl]))
    ma = jax.nn.silu(mg_ref[...]) * jnp.concatenate(heads, axis=-1)
    o_ref[...] = _out_proj_residual(h_ref[...], mo, ma, wo_ref, g_ref)


def _finish_prompt(h, a, gate, mq, mg, mkb, mvb, layer, batch, wo, g_post, tm):
    t = h.shape[0]
    n_mem = mkb.shape[1]
    blocks_per_batch = t // batch // tm
    row = lambda i: (i, 0)
    fixed = lambda i: (0, 0)
    wide = pl.BlockSpec((tm, MIX_W), row)
    memw = pl.BlockSpec((tm, MEM_W), row)
    memkv = pl.BlockSpec((1, n_mem, MEM_W), lambda i: (layer * batch + i // blocks_per_batch, 0, 0))
    gated = gate is not None
    ins = [h, a] + ([gate] if gated else []) + [mq, mg, mkb, mvb, wo, g_post]
    specs = [wide, wide] + ([wide] if gated else []) + [memw, memw, memkv, memkv,
                                                         pl.BlockSpec(wo.shape, fixed), pl.BlockSpec((1, D_MODEL), fixed)]
    return pl.pallas_call(
        functools.partial(_finish_kernel, gated=gated),
        grid=(t // tm,),
        in_specs=specs,
        out_specs=wide,
        out_shape=jax.ShapeDtypeStruct((t, D_MODEL), F32),
        compiler_params=_params("parallel"),
        name="finish_prompt",
    )(*ins)


def _mem_decode_kernel(q_ref, mk_ref, mv_ref, o_ref):
    rows = 8
    q = jnp.broadcast_to(q_ref[0], (rows, MEM_W))
    lane_head = lax.broadcasted_iota(jnp.int32, (rows, MEM_W), 1) // MEM_HD
    row = lax.broadcasted_iota(jnp.int32, (rows, MEM_W), 0)
    own = lane_head == row
    qbd = jnp.where(own, q, 0.0).astype(BF16)
    p = _softmax_rows(_dot_nt(qbd, mk_ref[0].astype(BF16)) * (MEM_HD ** -0.5))
    o = _dot(p.astype(BF16), mv_ref[0].astype(BF16))
    o_ref[0] = jnp.sum(jnp.where(own, o, 0.0), axis=0, keepdims=True)


def _mem_decode(mq, cache_mk, cache_mv, layer):
    bs = mq.shape[0]
    n_mem = cache_mk.shape[1]
    seq = pl.BlockSpec((1, 1, MEM_W), lambda b: (b, 0, 0))
    kv = pl.BlockSpec((1, n_mem, MEM_W), lambda b: (b, 0, layer))
    out = pl.pallas_call(
        _mem_decode_kernel,
        grid=(bs,),
        in_specs=[seq, kv, kv],
        out_specs=seq,
        out_shape=jax.ShapeDtypeStruct((bs, 1, MEM_W), F32),
        compiler_params=_params("parallel"),
        name="mem_decode",
    )(mq[:, None, :], cache_mk, cache_mv)
    return out[:, 0, :]


def _finish_single_kernel(*refs, gated):
    if gated:
        h_ref, a_ref, gate_ref, matt_ref, mg_ref, wo_ref, g_ref, o_ref = refs
        mo = jax.nn.silu(gate_ref[...]) * a_ref[...]
    else:
        h_ref, a_ref, matt_ref, mg_ref, wo_ref, g_ref, o_ref = refs
        mo = a_ref[...]
    ma = jax.nn.silu(mg_ref[...]) * matt_ref[...]
    o_ref[...] = _out_proj_residual(h_ref[...], mo, ma, wo_ref, g_ref)


def _finish_single(h, a, gate, matt, mg, wo, g_post):
    gated = gate is not None
    ins = [h, a] + ([gate] if gated else []) + [matt, mg, wo, g_post]
    full = lambda arr: pl.BlockSpec(arr.shape, lambda: (0,) * arr.ndim)
    return pl.pallas_call(
        functools.partial(_finish_single_kernel, gated=gated),
        in_specs=[full(x) for x in ins],
        out_specs=pl.BlockSpec(h.shape, lambda: (0, 0)),
        out_shape=jax.ShapeDtypeStruct(h.shape, F32),
        compiler_params=pltpu.CompilerParams(vmem_limit_bytes=VMEM_LIMIT),
        name="finish_single",
    )(*ins)


def _rope_tables(pos):
    inv = ROPE_THETA ** (-jnp.arange(ROT_HALF, dtype=F32) / ROT_HALF)
    ang = pos.astype(F32)[:, None] * inv[None, :]
    cos, sin = jnp.cos(ang), jnp.sin(ang)
    n = pos.shape[0]
    rest = DA_HD - ROT_DIM
    c64 = jnp.concatenate([cos, cos, jnp.ones((n, rest), F32)], axis=1)
    s64 = jnp.concatenate([-sin, sin, jnp.zeros((n, rest), F32)], axis=1)
    return jnp.tile(c64, (1, 2)), jnp.tile(s64, (1, 2))


def _pad_lanes(v):
    return jnp.pad(v, (0, LANES - v.shape[0]))[None, :]


def kernel(x_prompt, x_sample, cache_attn_k, cache_attn_v, cache_mem_k, cache_mem_v, state_ssm, state_conv, page_table, mem_prompt, norm_pre, norm_post, w_mem_kv, w_out, attn_w_in, attn_lambda, attn_subln, gm_w_in, gm_norm, gm_ws, gm_bias, ssd_w_in, ssd_conv_w, ssd_conv_b, ssd_dt_bias, ssd_a_log, ssd_d, ssd_norm):
    bp, sp, _ = x_prompt.shape
    bs, ts, _ = x_sample.shape
    assert ts == 1
    depth = norm_pre.shape[0]
    n_mem = mem_prompt.shape[1]
    page = cache_attn_k.shape[1]
    past = page_table.shape[1] * page
    tp = bp * sp
    tm = min(256, sp)
    tq = min(FLASH_TQ, sp)
    tk = tq // 2

    cos_p, sin_p = _rope_tables(jnp.arange(sp, dtype=jnp.int32))
    cos_s, sin_s = _rope_tables(jnp.full((bs,), past, jnp.int32))

    mk_all, mv_all, mkb, mvb = _memkv(mem_prompt.reshape(bp * n_mem, D_MODEL), w_mem_kv)
    mkb = mkb.reshape(depth * bp, n_mem, MEM_W)
    mvb = mvb.reshape(depth * bp, n_mem, MEM_W)
    cache_mk = cache_mem_k.reshape(bs, n_mem, depth * MEM_W)
    cache_mv = cache_mem_v.reshape(bs, n_mem, depth * MEM_W)

    hp = x_prompt.reshape(tp, D_MODEL)
    hs = x_sample.reshape(bs, D_MODEL)
    n_attn = attn_w_in.shape[0]
    kv_p = kv_s = None
    ssm_p, conv_p, ssm_s, conv_s, gm_v_s = [], [], [], [], []
    for i in range(depth):
        kind, j = i % 3, i // 3
        g_pre = norm_pre[i][None, :]
        g_post = norm_post[i][None, :]
        wo = w_out[i].astype(BF16)
        if kind == 0:
            lam_init = 0.8 - 0.6 * math.exp(-0.3 * i)
            w = attn_w_in[j].astype(BF16)
            subln = attn_subln[j][None, :]
            q, k, v, kb, vb, gate_p, mq_p, mg_p = _inproj_attn(hp, g_pre, w, cos_p, sin_p, tm, BF16, BF16,
                                                               j, n_attn, kv_p)
            kv_p = (k, v)
            a_p = _flash(attn_lambda[j], q.reshape(bp, sp, MIX_W), kb.reshape(bp, sp, MIX_W),
                         vb.reshape(bp, sp, MIX_W), subln, tq, tk, lam_init).reshape(tp, MIX_W)
            q, k, v, _, _, gate_s, mq_s, mg_s = _inproj_attn(hs, g_pre, w, cos_s, sin_s, bs, F32, F32,
                                                             j, n_attn, kv_s)
            kv_s = (k, v)
            a_s = _decode_attn(page_table, attn_lambda[j], q, k, v, cache_attn_k, cache_attn_v, subln, j,
                               lam_init).reshape(bs, MIX_W)
        elif kind == 1:
            w = gm_w_in[j].astype(BF16)
            gn = gm_norm[j][None, :]
            segs = lambda mq_dt: ((2 * MIX_W, F32), (MIX_W, F32), (MEM_W, mq_dt), (MEM_W, F32))
            main, gate_p, mq_p, mg_p = _inproj_plain(hp, g_pre, w, tm, segs(BF16), "inproj_gm")
            a_p = _gm_mixer(main, gn, gm_ws[j][:, :min(GM_CHUNK, sp), :min(GM_CHUNK, sp)],
                            gm_bias[j][:, :min(GM_CHUNK, sp)], tm)
            main, gate_s, mq_s, mg_s = _inproj_plain(hs, g_pre, w, bs, segs(F32), "inproj_gm")
            a_s, vn_s = _gm_single(main, gn, gm_ws[j], gm_bias[j])
            gm_v_s.append(vn_s.reshape(bs, ts, MIX_W))
        else:
            wf = ssd_w_in[j]
            n_dt = SSD_HEADS
            tail = SSD_CONV_DIM + n_dt
            w = jnp.concatenate([wf[:, :SSD_CONV_DIM], wf[:, tail:], wf[:, SSD_CONV_DIM:tail],
                                 jnp.zeros((D_MODEL, LANES - n_dt), F32)], axis=1).astype(BF16)
            segs = lambda mq_dt: ((SSD_CONV_DIM, F32), (MIX_W, F32), (MEM_W, mq_dt), (MEM_W, F32), (LANES, F32))
            pars = (ssd_conv_w[j], ssd_conv_b[j][None, :], _pad_lanes(ssd_dt_bias[j]), _pad_lanes(ssd_a_log[j]),
                    jnp.repeat(ssd_d[j], SSD_HD)[None, :], ssd_norm[j][None, :])
            xbc, z, mq_p, mg_p, dt = _inproj_plain(hp, g_pre, w, tm, segs(BF16), "inproj_ssd")
            a_p, st = _ssd_mixer(xbc, dt, z, bp, *pars)
            gate_p = None
            ssm_p.append(st.reshape(bp, SSD_HEADS, SSD_HD, SSD_N))
            conv_p.append(xbc.reshape(bp, sp, SSD_CONV_DIM)[:, sp - (SSD_CONV - 1):, :])
            xbc, z, mq_s, mg_s, dt = _inproj_plain(hs, g_pre, w, bs, segs(F32), "inproj_ssd")
            a_s, cn, sn = _ssd_single(xbc, dt, z, state_conv[j],
                                      state_ssm[j].reshape(bs, SSD_PAIRS, LANES, SSD_N), *pars)
            a_s = a_s.reshape(bs, MIX_W)
            gate_s = None
            ssm_s.append(sn.reshape(bs, SSD_HEADS, SSD_HD, SSD_N))
            conv_s.append(cn)
        hp = _finish_prompt(hp, a_p, gate_p, mq_p, mg_p, mkb, mvb, i, bp, wo, g_post, tm)
        matt = _mem_decode(mq_s, cache_mk, cache_mv, i)
        hs = _finish_single(hs, a_s, gate_s, matt, mg_s, wo, g_post)

    return (hp.reshape(bp, sp, D_MODEL), hs.reshape(bs, ts, D_MODEL),
            kv_p[0].reshape(bp, sp, n_attn, DA_HEADS, HEAD_W), kv_p[1].reshape(bp, sp, n_attn, DA_HEADS, HEAD_W),
            mk_all.reshape(bp, n_mem, depth, MEM_HEADS, MEM_HD), mv_all.reshape(bp, n_mem, depth, MEM_HEADS, MEM_HD),
            jnp.stack(ssm_p, axis=0), jnp.stack(conv_p, axis=0),
            kv_s[0].reshape(bs, ts, n_attn, DA_HEADS, HEAD_W), kv_s[1].reshape(bs, ts, n_attn, DA_HEADS, HEAD_W),
            jnp.stack(ssm_s, axis=0), jnp.stack(conv_s, axis=0),
            jnp.stack(gm_v_s, axis=2))
```

```python
import functools
import math

import jax
import jax.numpy as jnp
from jax import lax
from jax.experimental import pallas as pl
from jax.experimental.pallas import tpu as pltpu

F32 = jnp.float32
BF16 = jnp.bfloat16

D_MODEL = 1024
MIX_W = D_MODEL
MEM_HEADS = 4
MEM_HD = 128
MEM_W = MEM_HEADS * MEM_HD
DA_HEADS = 8
DA_HD = 64
HEAD_W = 2 * DA_HD
ROT_DIM = DA_HD // 4
ROT_HALF = ROT_DIM // 2
ROPE_THETA = 500000.0
GM_CHUNK = 128
GM_GROUPS = 8
SSD_HD = 64
SSD_HEADS = MIX_W // SSD_HD
SSD_GROUPS = 4
SSD_N = 128
SSD_CONV = 4
SSD_CHUNK = 128
SSD_PAIRS = SSD_HEADS // 2
SSD_CONV_DIM = MIX_W + 2 * SSD_GROUPS * SSD_N
EPS = 1e-6
NEG_BIG = -1e30
LANES = 128
VMEM_LIMIT = 56 * 1024 * 1024
FLASH_TQ = 512
FLASH_ROWS = 32
DECODE_PAGES = 8
Q_SCALE = DA_HD ** -0.5 * math.log2(math.e)
FINISH_TM = 512
FLASH_UNROLL = 1
FLASH_LANES = 256
FLASH_VROWS = 2 * DA_HD + 16

_NT = (((1,), (1,)), ((), ()))


def _params(*sem):
    return pltpu.CompilerParams(dimension_semantics=sem, vmem_limit_bytes=VMEM_LIMIT)


def _dot(a, b):
    return jnp.dot(a, b, preferred_element_type=F32)


def _dot_nt(a, b):
    return lax.dot_general(a, b, _NT, preferred_element_type=F32)


def _rms_scale(x, g):
    return x * lax.rsqrt(jnp.mean(x * x, axis=-1, keepdims=True) + EPS) * g


def _rope_head(x, c, s):
    lane = lax.broadcasted_iota(jnp.int32, x.shape, 1) & (DA_HD - 1)
    partner = jnp.where(lane < ROT_HALF, pltpu.roll(x, LANES - ROT_HALF, 1), pltpu.roll(x, ROT_HALF, 1))
    return x * c + partner * s


def _inproj_attn_kernel(x_ref, g_ref, w_ref, c_ref, s_ref, *refs, for_flash):
    if for_flash:
        q_ref, k_ref, v_ref, kb_ref, vt_ref, gate_ref, mq_ref, mg_ref = refs[-8:]
    else:
        q_ref, k_ref, v_ref, gate_ref, mq_ref, mg_ref = refs[-6:]
    xn = _rms_scale(x_ref[...], g_ref[...]).astype(BF16)
    c = c_ref[...]
    s = s_ref[...]
    q = _dot(xn, w_ref[:, 0:MIX_W])
    k = _dot(xn, w_ref[:, MIX_W:2 * MIX_W])
    v = _dot(xn, w_ref[:, 2 * MIX_W:3 * MIX_W])
    v_ref[...] = v
    for h in range(DA_HEADS):
        sl = slice(h * HEAD_W, (h + 1) * HEAD_W)
        q_ref[:, sl] = (_rope_head(q[:, sl], c, s) * Q_SCALE).astype(q_ref.dtype)
        kr = _rope_head(k[:, sl], c, s)
        k_ref[:, sl] = kr
        if for_flash:
            kb_ref[:, sl] = kr.astype(BF16)
            vt_ref[0, h * FLASH_VROWS:h * FLASH_VROWS + HEAD_W, :] = v[:, sl].T.astype(BF16)
            vt_ref[0, h * FLASH_VROWS + HEAD_W:(h + 1) * FLASH_VROWS, :] = jnp.ones(
                (FLASH_VROWS - HEAD_W, v.shape[0]), BF16)
    o = 3 * MIX_W
    gate_ref[...] = _dot(xn, w_ref[:, o:o + MIX_W])
    mq_ref[...] = _dot(xn, w_ref[:, o + MIX_W:o + MIX_W + MEM_W]).astype(mq_ref.dtype)
    mg_ref[...] = _dot(xn, w_ref[:, o + MIX_W + MEM_W:o + MIX_W + 2 * MEM_W])


def _inproj_attn(x, g, w, cos_t, sin_t, tm, flash_batch, slot, n_slots, kv_all):
    t = x.shape[0]
    for_flash = flash_batch is not None
    n_pos_blocks = cos_t.shape[0] // tm
    row = lambda i: (i, 0)
    fixed = lambda i: (0, 0)
    pos = lambda i: (i % n_pos_blocks, 0)
    wide = pl.BlockSpec((tm, MIX_W), row)
    mem = pl.BlockSpec((tm, MEM_W), row)
    kv_out = pl.BlockSpec((tm, MIX_W), lambda i: (i, slot))
    kv_shape = jax.ShapeDtypeStruct((t, n_slots * MIX_W), F32)
    ins = [x, g, w, cos_t, sin_t]
    in_specs = [wide, pl.BlockSpec((1, D_MODEL), fixed), pl.BlockSpec(w.shape, fixed),
                pl.BlockSpec((tm, HEAD_W), pos), pl.BlockSpec((tm, HEAD_W), pos)]
    aliases = {}
    if kv_all is not None:
        aliases = {len(ins): 1, len(ins) + 1: 2}
        ins += list(kv_all)
        in_specs += [pl.BlockSpec(memory_space=pl.ANY)] * 2
    act = BF16 if for_flash else F32
    out_specs = [wide, kv_out, kv_out]
    out_shape = [jax.ShapeDtypeStruct((t, MIX_W), act), kv_shape, kv_shape]
    if for_flash:
        tiles = t // flash_batch // tm
        out_specs += [wide, pl.BlockSpec((1, DA_HEADS * FLASH_VROWS, tm), lambda i: (i // tiles, 0, i % tiles))]
        out_shape += [jax.ShapeDtypeStruct((t, MIX_W), BF16),
                      jax.ShapeDtypeStruct((flash_batch, DA_HEADS * FLASH_VROWS, t // flash_batch), BF16)]
    out_specs += [wide, mem, mem]
    out_shape += [jax.ShapeDtypeStruct((t, MIX_W), F32),
                  jax.ShapeDtypeStruct((t, MEM_W), act), jax.ShapeDtypeStruct((t, MEM_W), F32)]
    return pl.pallas_call(
        functools.partial(_inproj_attn_kernel, for_flash=for_flash),
        grid=(t // tm,),
        in_specs=in_specs,
        out_specs=out_specs,
        out_shape=out_shape,
        input_output_aliases=aliases,
        compiler_params=_params("parallel"),
        name="inproj_attn",
    )(*ins)


def _inproj_plain_kernel(x_ref, g_ref, w_ref, *out_refs, starts):
    xn = _rms_scale(x_ref[...], g_ref[...]).astype(BF16)
    for ref, start in zip(out_refs, starts):
        ref[...] = _dot(xn, w_ref[:, start:start + ref.shape[1]]).astype(ref.dtype)


def _inproj_plain(x, g, w, tm, segments, name):
    t = x.shape[0]
    starts, acc = [], 0
    for width, _ in segments:
        starts.append(acc)
        acc += width
    assert acc == w.shape[1]
    row = lambda i: (i, 0)
    fixed = lambda i: (0, 0)
    return pl.pallas_call(
        functools.partial(_inproj_plain_kernel, starts=tuple(starts)),
        grid=(t // tm,),
        in_specs=[pl.BlockSpec((tm, D_MODEL), row), pl.BlockSpec((1, D_MODEL), fixed),
                  pl.BlockSpec(w.shape, fixed)],
        out_specs=[pl.BlockSpec((tm, width), row) for width, _ in segments],
        out_shape=[jax.ShapeDtypeStruct((t, width), dt) for width, dt in segments],
        compiler_params=_params("parallel"),
        name=name,
    )(x, g, w)


def _memkv_kernel(x_ref, w_ref, mk_ref, mv_ref, mkb_ref, mvb_ref):
    kv = _dot(x_ref[...].astype(BF16), w_ref[0].astype(BF16))
    k = kv[:, :MEM_W]
    v = kv[:, MEM_W:]
    mk_ref[...] = k
    mv_ref[...] = v
    mkb_ref[0] = k.astype(BF16)
    mvb_ref[0] = v.astype(BF16)


def _memkv(mem_tokens, w_mem_kv):
    tm = mem_tokens.shape[0]
    depth = w_mem_kv.shape[0]
    f32_out = pl.BlockSpec((tm, MEM_W), lambda i: (0, i))
    bf_out = pl.BlockSpec((1, tm, MEM_W), lambda i: (i, 0, 0))
    return pl.pallas_call(
        _memkv_kernel,
        grid=(depth,),
        in_specs=[pl.BlockSpec((tm, D_MODEL), lambda i: (0, 0)),
                  pl.BlockSpec((1, D_MODEL, 2 * MEM_W), lambda i: (i, 0, 0))],
        out_specs=[f32_out, f32_out, bf_out, bf_out],
        out_shape=[jax.ShapeDtypeStruct((tm, depth * MEM_W), F32), jax.ShapeDtypeStruct((tm, depth * MEM_W), F32),
                   jax.ShapeDtypeStruct((depth, tm, MEM_W), BF16), jax.ShapeDtypeStruct((depth, tm, MEM_W), BF16)],
        compiler_params=_params("parallel"),
        name="memkv",
    )(mem_tokens, w_mem_kv)


def _diff_lambda(lp, lam_init):
    a = jnp.sum(lp[0:1, :] * lp[1:2, :], axis=-1, keepdims=True)
    b = jnp.sum(lp[2:3, :] * lp[3:4, :], axis=-1, keepdims=True)
    return jnp.exp(a) - jnp.exp(b) + lam_init


def _sub_norm(o, g, lam_init):
    return _rms_scale(o, g) * (1.0 - lam_init)


def _flash_kernel(lam_ref, q_ref, k_ref, vt_ref, g_ref, o_ref,
                  qqt_sc, s_sc, mx_sc, p_sc, al_sc, m_sc, acc_sc, *, tq, tk, lam_init):
    assert tq == 2 * tk
    i = pl.program_id(2)
    qt = q_ref[0].astype(F32).T
    feat = lax.broadcasted_iota(jnp.int32, qt.shape, 0)
    qqt_sc[...] = jnp.concatenate([jnp.where(feat < DA_HD, qt, 0.0), jnp.where(feat >= DA_HD, qt, 0.0)],
                                  axis=1).astype(BF16)
    m_sc[...] = jnp.full(m_sc.shape, NEG_BIG, F32)
    acc_sc[...] = jnp.zeros(acc_sc.shape, F32)
    p_sc[1] = jnp.zeros(p_sc.shape[1:], BF16)
    al_sc[1] = jnp.ones(al_sc.shape[1:], F32)

    assert FLASH_LANES == tk
    lane_groups = [slice(c * FLASH_LANES, (c + 1) * FLASH_LANES) for c in range(2 * tq // FLASH_LANES)]
    every = tuple(range(len(lane_groups)))

    def scores(blk, slot, groups=every):
        start = pl.multiple_of(blk * tk, tk)
        k = k_ref[0, pl.ds(start, tk), :]
        for g in groups:
            cols = lane_groups[g]
            st = _dot(k, qqt_sc[:, cols])
            s_sc[slot, :, cols] = st
            mx_sc[slot, :, cols] = jnp.max(st, axis=0, keepdims=True)

    n = 2 * tq
    chunks = [slice(c * FLASH_ROWS, (c + 1) * FLASH_ROWS) for c in range(tk // FLASH_ROWS)]

    def probs(slot):
        m_old = m_sc[...]
        m_new = jnp.maximum(m_old, mx_sc[slot])
        al_sc[slot] = jnp.exp2(m_old - m_new)
        m_sc[...] = m_new
        for rows in chunks:
            p_sc[slot, rows, :] = jnp.exp2(s_sc[slot, rows, :] - m_new).astype(BF16)

    def probs_diag(slot, groups, causal):
        for g in groups:
            cols = lane_groups[g]
            if g in causal:
                part = jnp.full((8, FLASH_LANES), NEG_BIG, F32)
                for rows in chunks:
                    r = rows.start + lax.broadcasted_iota(jnp.int32, (FLASH_ROWS, FLASH_LANES), 0)
                    c = lax.broadcasted_iota(jnp.int32, (FLASH_ROWS, FLASH_LANES), 1)
                    st = jnp.where(r <= c, s_sc[slot, rows, cols], NEG_BIG)
                    s_sc[slot, rows, cols] = st
                    part = jnp.maximum(part, jnp.max(st.reshape(FLASH_ROWS // 8, 8, FLASH_LANES), axis=0))
                mx = jnp.max(part, axis=0, keepdims=True)
            else:
                mx = mx_sc[slot, :, cols]
            m_old = m_sc[:, cols]
            m_new = jnp.maximum(m_old, mx)
            al_sc[slot, :, cols] = jnp.exp2(m_old - m_new)
            m_sc[:, cols] = m_new
            for rows in chunks:
                p_sc[slot, rows, cols] = jnp.exp2(s_sc[slot, rows, cols] - m_new).astype(BF16)

    def values(blk, slot, groups=every):
        start = pl.multiple_of(blk * tk, tk)
        vt = vt_ref[0, :, pl.ds(start, tk)]
        for g in groups:
            cols = lane_groups[g]
            pv = _dot(vt, p_sc[slot, :, cols])
            bits = pltpu.bitcast(pv[0:8, :], jnp.uint32)
            zero = lax.shift_right_logical(lax.shift_right_logical(bits, jnp.uint32(16)), jnp.uint32(16))
            al = al_sc[slot, :, cols] + zero[0:1, :].astype(F32)
            acc_sc[:, cols] = acc_sc[:, cols] * al + pv

    def pair(t):
        values(jnp.maximum(2 * t - 1, 0), 1)
        scores(2 * t + 1, 1)
        probs(0)
        values(2 * t, 0)
        scores(2 * t + 2, 0)
        probs(1)

    def pairs(u, carry):
        for r in range(FLASH_UNROLL):
            pair(FLASH_UNROLL * u + r)
        return carry

    scores(0, 0)
    lax.fori_loop(0, i // FLASH_UNROLL, pairs, 0)
    for r in range(FLASH_UNROLL - 1):
        @pl.when(i % FLASH_UNROLL > r)
        def _():
            pair(i - i % FLASH_UNROLL + r)
    first_half, second_half = (0, 2), (1, 3)
    values(jnp.maximum(2 * i - 1, 0), 1)
    scores(2 * i + 1, 1, second_half)
    probs_diag(0, every, first_half)
    values(2 * i, 0)
    probs_diag(1, second_half, second_half)
    values(2 * i + 1, 1, second_half)

    o = acc_sc[0:HEAD_W, :] / acc_sc[HEAD_W:HEAD_W + 1, :]
    lam = _diff_lambda(lam_ref[...], lam_init)
    od = o[:, :tq] - lam * o[:, tq:]
    scale = lax.rsqrt(jnp.mean(od * od, axis=0, keepdims=True) + EPS)
    o_ref[0] = (od * scale * (g_ref[...] * (1.0 - lam_init))).T


def _flash(lam_p, q, kb, vt, subln, tq, tk, lam_init):
    b, s, _ = q.shape
    assert tq & (tq - 1) == 0 and tq == 2 * tk and s % tq == 0
    n = 2 * tq
    qo = pl.BlockSpec((1, tq, HEAD_W), lambda bi, h, i: (bi, i, h))
    return pl.pallas_call(
        functools.partial(_flash_kernel, tq=tq, tk=tk, lam_init=lam_init),
        grid=(b, DA_HEADS, s // tq),
        in_specs=[pl.BlockSpec((4, DA_HD), lambda bi, h, i: (0, 0)), qo,
                  pl.BlockSpec((1, s, HEAD_W), lambda bi, h, i: (bi, 0, h)),
                  pl.BlockSpec((1, FLASH_VROWS, s), lambda bi, h, i: (bi, h, 0)),
                  pl.BlockSpec((HEAD_W, 1), lambda bi, h, i: (0, 0))],
        out_specs=qo,
        out_shape=jax.ShapeDtypeStruct((b, s, MIX_W), F32),
        scratch_shapes=[pltpu.VMEM((HEAD_W, n), BF16),
                        pltpu.VMEM((2, tk, n), F32), pltpu.VMEM((2, 1, n), F32),
                        pltpu.VMEM((2, tk, n), BF16), pltpu.VMEM((2, 1, n), F32),
                        pltpu.VMEM((1, n), F32), pltpu.VMEM((FLASH_VROWS, n), F32)],
        compiler_params=_params("parallel", "parallel", "arbitrary"),
        name="flash_diff",
    )(lam_p, q, kb, vt, subln.reshape(HEAD_W, 1))


def _decode_kernel(pt_ref, lam_ref, q_ref, kn_ref, vn_ref, *refs, n_steps, pps, lam_init):
    del pt_ref
    kc_refs, vc_refs = refs[:pps], refs[pps:2 * pps]
    g_ref, o_ref, m_sc, l_sc, acc_sc = refs[2 * pps:]
    step = pl.program_id(1)
    q8 = q_ref[0]
    lane = lax.broadcasted_iota(jnp.int32, q8.shape, 1)
    zero = jnp.zeros_like(q8)
    qsel = jnp.concatenate([jnp.where(lane < DA_HD, q8, zero), jnp.where(lane >= DA_HD, q8, zero)],
                           axis=0).astype(BF16)

    @pl.when(step == 0)
    def _():
        m_sc[...] = jnp.full(m_sc.shape, NEG_BIG, F32)
        l_sc[...] = jnp.zeros(l_sc.shape, F32)
        acc_sc[...] = jnp.zeros(acc_sc.shape, F32)

    rows = kc_refs[0].shape[1] * DA_HEADS
    col = lax.broadcasted_iota(jnp.int32, (2 * DA_HEADS, rows), 1)
    row = lax.broadcasted_iota(jnp.int32, (2 * DA_HEADS, rows), 0)
    same_head = (col & (DA_HEADS - 1)) == (row & (DA_HEADS - 1))
    scores = [jnp.where(same_head, _dot_nt(qsel, kc[0, :, 0].reshape(rows, HEAD_W).astype(BF16)), NEG_BIG)
              for kc in kc_refs]
    m_old = m_sc[...]
    m_new = m_old
    for s in scores:
        m_new = jnp.maximum(m_new, jnp.max(s, axis=-1, keepdims=True))
    alpha = jnp.exp2(m_old - m_new)
    l_new = alpha * l_sc[...]
    acc_new = alpha * acc_sc[...]
    for s, vc in zip(scores, vc_refs):
        pm = jnp.exp2(s - m_new)
        l_new = l_new + jnp.sum(pm, axis=-1, keepdims=True)
        acc_new = acc_new + _dot(pm.astype(BF16), vc[0, :, 0].reshape(rows, HEAD_W).astype(BF16))
    m_sc[...] = m_new
    l_sc[...] = l_new
    acc_sc[...] = acc_new

    @pl.when(step == n_steps - 1)
    def _():
        kn = kn_ref[0].astype(BF16).astype(F32)
        vn = vn_ref[0].astype(BF16).astype(F32)
        kn2 = jnp.concatenate([kn, kn], axis=0)
        vn2 = jnp.concatenate([vn, vn], axis=0)
        s_n = jnp.sum(qsel.astype(F32) * kn2, axis=-1, keepdims=True)
        m2 = jnp.maximum(m_new, s_n)
        a2 = jnp.exp2(m_new - m2)
        pn = jnp.exp2(s_n - m2)
        l2 = a2 * l_new + pn
        acc2 = a2 * acc_new + pn.astype(BF16).astype(F32) * vn2
        o = acc2 / l2
        lam = _diff_lambda(lam_ref[...], lam_init)
        od = o[:DA_HEADS] - lam * o[DA_HEADS:]
        o_ref[0] = _sub_norm(od, g_ref[...], lam_init)


def _decode_attn(page_table, lam_p, q, k_new, v_new, cache_k, cache_v, subln, layer, lam_init):
    bs, n_pages = page_table.shape
    page = cache_k.shape[1]
    pps = math.gcd(n_pages, DECODE_PAGES)
    per_seq = pl.BlockSpec((1, DA_HEADS, HEAD_W), lambda b, p, pt: (b, 0, 0))
    new_kv = pl.BlockSpec((1, DA_HEADS, HEAD_W), lambda b, p, pt: (b, layer, 0))
    cache = [pl.BlockSpec((1, page, 1, DA_HEADS, HEAD_W),
                          lambda b, p, pt, r=r: (pt[b * n_pages + p * pps + r], 0, layer, 0, 0)) for r in range(pps)]
    return pl.pallas_call(
        functools.partial(_decode_kernel, n_steps=n_pages // pps, pps=pps, lam_init=lam_init),
        grid_spec=pltpu.PrefetchScalarGridSpec(
            num_scalar_prefetch=1,
            grid=(bs, n_pages // pps),
            in_specs=[pl.BlockSpec((4, DA_HD), lambda b, p, pt: (0, 0)), per_seq, new_kv, new_kv, *cache, *cache,
                      pl.BlockSpec((1, HEAD_W), lambda b, p, pt: (0, 0))],
            out_specs=per_seq,
            scratch_shapes=[pltpu.VMEM((2 * DA_HEADS, 1), F32), pltpu.VMEM((2 * DA_HEADS, 1), F32),
                            pltpu.VMEM((2 * DA_HEADS, HEAD_W), F32)],
        ),
        out_shape=jax.ShapeDtypeStruct((bs, DA_HEADS, HEAD_W), F32),
        compiler_params=_params("parallel", "arbitrary"),
        name="decode_attn",
    )(page_table.reshape(-1), lam_p, q.reshape(bs, DA_HEADS, HEAD_W), k_new.reshape(bs, -1, HEAD_W),
      v_new.reshape(bs, -1, HEAD_W), *([cache_k] * pps), *([cache_v] * pps), subln)


def _layernorm(x, g):
    xc = x - jnp.mean(x, axis=-1, keepdims=True)
    return xc * lax.rsqrt(jnp.mean(xc * xc, axis=-1, keepdims=True) + EPS) * g


def _gm_kernel(main_ref, gn_ref, ws_ref, bt_ref, y_ref, *, n_chunks):
    q = ws_ref.shape[1]
    r = lax.broadcasted_iota(jnp.int32, (q, q), 0)
    c = lax.broadcasted_iota(jnp.int32, (q, q), 1)
    wm = [jnp.where(c <= r, ws_ref[g], 0.0).astype(BF16) for g in range(GM_GROUPS)]
    for ch in range(n_chunks):
        rows = slice(ch * q, (ch + 1) * q)
        u = jax.nn.gelu(main_ref[rows, 0:MIX_W])
        vn = _layernorm(jax.nn.gelu(main_ref[rows, MIX_W:2 * MIX_W]), gn_ref[...]).astype(BF16)
        for g in range(GM_GROUPS):
            sl = slice(g * LANES, (g + 1) * LANES)
            sv = _dot(wm[g], vn[:, sl]) + bt_ref[:, g:g + 1]
            y_ref[rows, sl] = u[:, sl] * sv


def _gm_mixer(main, gn, ws, bias, tm):
    t = main.shape[0]
    q = ws.shape[1]
    bias_t = bias.T
    fixed2 = lambda i: (0, 0)
    return pl.pallas_call(
        functools.partial(_gm_kernel, n_chunks=tm // q),
        grid=(t // tm,),
        in_specs=[pl.BlockSpec((tm, 2 * MIX_W), lambda i: (i, 0)), pl.BlockSpec((1, MIX_W), fixed2),
                  pl.BlockSpec(ws.shape, lambda i: (0, 0, 0)), pl.BlockSpec(bias_t.shape, fixed2)],
        out_specs=pl.BlockSpec((tm, MIX_W), lambda i: (i, 0)),
        out_shape=jax.ShapeDtypeStruct((t, MIX_W), F32),
        compiler_params=_params("parallel"),
        name="gm_mixer",
    )(main, gn, ws, bias_t)


def _gm_single_kernel(main_ref, gn_ref, w0_ref, b0_ref, y_ref, vn_ref):
    u = jax.nn.gelu(main_ref[:, 0:MIX_W])
    vn = _layernorm(jax.nn.gelu(main_ref[:, MIX_W:2 * MIX_W]), gn_ref[...])
    vn_ref[...] = vn
    y_ref[...] = u * (w0_ref[...] * vn + b0_ref[...])


def _gm_single(main, gn, ws, bias):
    t = main.shape[0]
    gd = MIX_W // GM_GROUPS
    w0 = jnp.repeat(ws[:, 0, 0], gd)[None, :]
    b0 = jnp.repeat(bias[:, 0], gd)[None, :]
    full = lambda shape: pl.BlockSpec(shape, lambda: (0,) * len(shape))
    return pl.pallas_call(
        _gm_single_kernel,
        in_specs=[full(main.shape), full(gn.shape), full(w0.shape), full(b0.shape)],
        out_specs=[full((t, MIX_W)), full((t, MIX_W))],
        out_shape=[jax.ShapeDtypeStruct((t, MIX_W), F32), jax.ShapeDtypeStruct((t, MIX_W), F32)],
        name="gm_single",
    )(main, gn, w0, b0)


def _pair_lanes(col_lo, col_hi, lo_mask):
    return jnp.where(lo_mask, col_lo, col_hi)


def _group_norm_store(y_ref, rows, g, ys, ng_ref):
    gw = MIX_W // SSD_GROUPS
    yg = jnp.concatenate(ys, axis=-1)
    y_ref[rows, g * gw:(g + 1) * gw] = _rms_scale(yg, ng_ref[:, g * gw:(g + 1) * gw])


def _ssd_kernel(xbc_ref, dt_ref, z_ref, cw_ref, cb_ref, dtb_ref, alog_ref, dsk_ref, ng_ref,
                y_ref, st_ref, xfull_sc):
    q = xbc_ref.shape[0]
    pad = 8

    @pl.when(pl.program_id(1) == 0)
    def _():
        st_ref[...] = jnp.zeros(st_ref.shape, F32)
        xfull_sc[0:pad, :] = jnp.zeros((pad, SSD_CONV_DIM), F32)

    x = xbc_ref[...]
    xfull_sc[pad:pad + q, :] = x
    conv = cb_ref[...] + cw_ref[SSD_CONV - 1:SSD_CONV, :] * x
    for tap in range(SSD_CONV - 1):
        back = SSD_CONV - 1 - tap
        conv = conv + cw_ref[tap:tap + 1, :] * xfull_sc[pad - back:pad - back + q, :]
    xfull_sc[0:pad, :] = x[q - pad:q, :]
    act = jax.nn.silu(conv)

    dtv = jax.nn.softplus(dt_ref[...] + dtb_ref[...])
    d_a = dtv * (-jnp.exp(alog_ref[...]))
    r = lax.broadcasted_iota(jnp.int32, (q, q), 0)
    c = lax.broadcasted_iota(jnp.int32, (q, q), 1)
    tril = c <= r
    cum = jnp.dot(tril.astype(F32), d_a, preferred_element_type=F32, precision=lax.Precision.HIGHEST)
    cum_t = cum.T
    dt_t = dtv.T
    cum_last = cum[q - 1:q, :]
    w_end = jnp.exp(cum_last - cum) * dtv
    e_cum = jnp.exp(cum)
    c_dec = jnp.exp(cum_last)

    lo = lax.broadcasted_iota(jnp.int32, (q, LANES), 1) < SSD_HD
    top = lax.broadcasted_iota(jnp.int32, (LANES, SSD_N), 0) < SSD_HD
    rows = slice(0, q)
    for g in range(SSD_GROUPS):
        bsl = slice(MIX_W + g * SSD_N, MIX_W + (g + 1) * SSD_N)
        csl = slice(MIX_W + SSD_GROUPS * SSD_N + g * SSD_N, MIX_W + SSD_GROUPS * SSD_N + (g + 1) * SSD_N)
        bcg = act[:, bsl].astype(BF16)
        ccg = act[:, csl].astype(BF16)
        cb = _dot_nt(ccg, bcg)
        ys = []
        for pp in range(SSD_PAIRS // SSD_GROUPS):
            p = g * (SSD_PAIRS // SSD_GROUPS) + pp
            h0, h1 = 2 * p, 2 * p + 1
            sl = slice(p * LANES, (p + 1) * LANES)
            xp = act[:, sl]
            xpb = xp.astype(BF16)
            zero = jnp.zeros_like(xpb)
            yd = None
            for h, xm in ((h0, jnp.where(lo, xpb, zero)), (h1, jnp.where(lo, zero, xpb))):
                seg = cum[:, h:h + 1] - cum_t[h:h + 1, :]
                dec = jnp.exp(jnp.where(tril, seg, -jnp.inf))
                w = (cb * dec * dt_t[h:h + 1, :]).astype(BF16)
                part = _dot(w, xm)
                yd = part if yd is None else yd + part
            state = st_ref[0, p]
            yo = _dot_nt(ccg, state.astype(BF16)) * _pair_lanes(e_cum[:, h0:h0 + 1], e_cum[:, h1:h1 + 1], lo)
            xw = xp * _pair_lanes(w_end[:, h0:h0 + 1], w_end[:, h1:h1 + 1], lo)
            upd = _dot(xw.T.astype(BF16), bcg)
            dec_rows = jnp.where(top, jnp.broadcast_to(c_dec[:, h0:h0 + 1], top.shape),
                                 jnp.broadcast_to(c_dec[:, h1:h1 + 1], top.shape))
            st_ref[0, p] = state * dec_rows + upd
            y = yd + yo + dsk_ref[:, sl] * xp
            ys.append(y * jax.nn.silu(z_ref[:, sl]))
        _group_norm_store(y_ref, rows, g, ys, ng_ref)


def _ssd_mixer(xbc, dt, z, batch, cw, cb, dtb, alog, dsk, ng):
    t = xbc.shape[0]
    q = min(SSD_CHUNK, t // batch)
    nc = t // batch // q
    tok = lambda b, c: (b * nc + c, 0)
    fixed = lambda b, c: (0, 0)
    par = lambda a: pl.BlockSpec(a.shape, fixed)
    return pl.pallas_call(
        _ssd_kernel,
        grid=(batch, nc),
        in_specs=[pl.BlockSpec((q, SSD_CONV_DIM), tok), pl.BlockSpec((q, LANES), tok), pl.BlockSpec((q, MIX_W), tok),
                  par(cw), par(cb), par(dtb), par(alog), par(dsk), par(ng)],
        out_specs=[pl.BlockSpec((q, MIX_W), tok),
                   pl.BlockSpec((1, SSD_PAIRS, LANES, SSD_N), lambda b, c: (b, 0, 0, 0))],
        out_shape=[jax.ShapeDtypeStruct((t, MIX_W), F32),
                   jax.ShapeDtypeStruct((batch, SSD_PAIRS, LANES, SSD_N), F32)],
        scratch_shapes=[pltpu.VMEM((8 + q, SSD_CONV_DIM), F32)],
        compiler_params=_params("parallel", "arbitrary"),
        name="ssd_mixer",
    )(xbc, dt, z, cw, cb, dtb, alog, dsk, ng)


def _ssd_single_kernel(xbc_ref, dt_ref, z_ref, cs_ref, st_ref, cw_ref, cb_ref, dtb_ref, alog_ref, dsk_ref, ng_ref,
                       y_ref, cn_ref, sn_ref):
    x = xbc_ref[0]
    buf = cs_ref[0]
    conv = cb_ref[...] + cw_ref[SSD_CONV - 1:SSD_CONV, :] * x
    for tap in range(SSD_CONV - 1):
        conv = conv + cw_ref[tap:tap + 1, :] * buf[tap:tap + 1, :]
    cn_ref[0, 0:SSD_CONV - 2, :] = buf[1:SSD_CONV - 1, :]
    cn_ref[0, SSD_CONV - 2:SSD_CONV - 1, :] = x
    act = jax.nn.silu(conv)

    dtv = jax.nn.softplus(dt_ref[0] + dtb_ref[...])
    e_da = jnp.exp(dtv * (-jnp.exp(alog_ref[...])))
    lo = lax.broadcasted_iota(jnp.int32, (1, LANES), 1) < SSD_HD
    top = lax.broadcasted_iota(jnp.int32, (LANES, SSD_N), 0) < SSD_HD
    z = z_ref[0]
    rows = slice(0, 1)
    for g in range(SSD_GROUPS):
        bsl = slice(MIX_W + g * SSD_N, MIX_W + (g + 1) * SSD_N)
        csl = slice(MIX_W + SSD_GROUPS * SSD_N + g * SSD_N, MIX_W + SSD_GROUPS * SSD_N + (g + 1) * SSD_N)
        bcg = act[:, bsl]
        ccg = act[:, csl]
        cbg = jnp.sum(ccg.astype(BF16).astype(F32) * bcg.astype(BF16).astype(F32), axis=-1, keepdims=True)
        cc8 = jnp.broadcast_to(ccg, (8, SSD_N)).astype(BF16)
        ys = []
        for pp in range(SSD_PAIRS // SSD_GROUPS):
            p = g * (SSD_PAIRS // SSD_GROUPS) + pp
            h0, h1 = 2 * p, 2 * p + 1
            sl = slice(p * LANES, (p + 1) * LANES)
            xp = act[:, sl]
            dtp = _pair_lanes(dtv[:, h0:h0 + 1], dtv[:, h1:h1 + 1], lo)
            ep = _pair_lanes(e_da[:, h0:h0 + 1], e_da[:, h1:h1 + 1], lo)
            state = st_ref[0, p]
            yo = _dot_nt(cc8, state.astype(BF16))[0:1, :] * ep
            xw = xp * dtp
            x_rows = jnp.broadcast_to(xw, (LANES, LANES)).T
            dec_rows = jnp.where(top, jnp.broadcast_to(e_da[:, h0:h0 + 1], top.shape),
                                 jnp.broadcast_to(e_da[:, h1:h1 + 1], top.shape))
            sn_ref[0, p] = state * dec_rows + x_rows * bcg
            y = cbg * xw + yo + dsk_ref[:, sl] * xp
            ys.append(y * jax.nn.silu(z[:, sl]))
        _group_norm_store(y_ref.at[0], rows, g, ys, ng_ref)


def _ssd_single(xbc, dt, z, conv_state, ssm_state, cw, cb, dtb, alog, dsk, ng):
    bs = xbc.shape[0]
    seq3 = lambda width: pl.BlockSpec((1, 1, width), lambda b: (b, 0, 0))
    par = lambda a: pl.BlockSpec(a.shape, lambda b: (0, 0))
    conv_spec = pl.BlockSpec((1, SSD_CONV - 1, SSD_CONV_DIM), lambda b: (b, 0, 0))
    st_spec = pl.BlockSpec((1, SSD_PAIRS, LANES, SSD_N), lambda b: (b, 0, 0, 0))
    return pl.pallas_call(
        _ssd_single_kernel,
        grid=(bs,),
        in_specs=[seq3(SSD_CONV_DIM), seq3(LANES), seq3(MIX_W), conv_spec, st_spec,
                  par(cw), par(cb), par(dtb), par(alog), par(dsk), par(ng)],
        out_specs=[seq3(MIX_W), conv_spec, st_spec],
        out_shape=[jax.ShapeDtypeStruct((bs, 1, MIX_W), F32),
                   jax.ShapeDtypeStruct(conv_state.shape, F32),
                   jax.ShapeDtypeStruct(ssm_state.shape, F32)],
        compiler_params=_params("parallel"),
        name="ssd_single",
    )(xbc[:, None, :], dt[:, None, :], z[:, None, :], conv_state, ssm_state, cw, cb, dtb, alog, dsk, ng)


def _softmax_rows(s):
    e = jnp.exp(s - jnp.max(s, axis=-1, keepdims=True))
    return e / jnp.sum(e, axis=-1, keepdims=True)


def _out_proj_residual(h, mo, ma, wo_ref, g_ref):
    out = _dot(mo.astype(BF16), wo_ref[0:MIX_W, :]) + _dot(ma.astype(BF16), wo_ref[MIX_W:MIX_W + MEM_W, :])
    return h + _rms_scale(out, g_ref[...])


def _finish_kernel(*refs, gated):
    if gated:
        h_ref, a_ref, gate_ref, mq_ref, mg_ref, mk_ref, mv_ref, wo_ref, g_ref, o_ref = refs
        mo = jax.nn.silu(gate_ref[...]) * a_ref[...]
    else:
        h_ref, a_ref, mq_ref, mg_ref, mk_ref, mv_ref, wo_ref, g_ref, o_ref = refs
        mo = a_ref[...]
    heads = []
    for hd in range(MEM_HEADS):
        sl = slice(hd * MEM_HD, (hd + 1) * MEM_HD)
        p = _softmax_rows(_dot_nt(mq_ref[:, sl], mk_ref[0, :, sl]) * (MEM_HD ** -0.5))
        heads.append(_dot(p.astype(BF16), mv_ref[0, :, sl]))
    ma = jax.nn.silu(mg_ref[...]) * jnp.concatenate(heads, axis=-1)
    o_ref[...] = _out_proj_residual(h_ref[...], mo, ma, wo_ref, g_ref)


def _finish_prompt(h, a, gate, mq, mg, mkb, mvb, layer, batch, wo, g_post, tm):
    t = h.shape[0]
    n_mem = mkb.shape[1]
    blocks_per_batch = t // batch // tm
    row = lambda i: (i, 0)
    fixed = lambda i: (0, 0)
    wide = pl.BlockSpec((tm, MIX_W), row)
    memw = pl.BlockSpec((tm, MEM_W), row)
    memkv = pl.BlockSpec((1, n_mem, MEM_W), lambda i: (layer * batch + i // blocks_per_batch, 0, 0))
    gated = gate is not None
    ins = [h, a] + ([gate] if gated else []) + [mq, mg, mkb, mvb, wo, g_post]
    specs = [wide, wide] + ([wide] if gated else []) + [memw, memw, memkv, memkv,
                                                         pl.BlockSpec(wo.shape, fixed), pl.BlockSpec((1, D_MODEL), fixed)]
    return pl.pallas_call(
        functools.partial(_finish_kernel, gated=gated),
        grid=(t // tm,),
        in_specs=specs,
        out_specs=wide,
        out_shape=jax.ShapeDtypeStruct((t, D_MODEL), F32),
        compiler_params=_params("parallel"),
        name="finish_prompt",
    )(*ins)


def _mem_decode_kernel(q_ref, mk_ref, mv_ref, o_ref):
    rows = 8
    q = jnp.broadcast_to(q_ref[0], (rows, MEM_W))
    lane_head = lax.broadcasted_iota(jnp.int32, (rows, MEM_W), 1) // MEM_HD
    row = lax.broadcasted_iota(jnp.int32, (rows, MEM_W), 0)
    own = lane_head == row
    qbd = jnp.where(own, q, 0.0).astype(BF16)
    p = _softmax_rows(_dot_nt(qbd, mk_ref[0].astype(BF16)) * (MEM_HD ** -0.5))
    o = _dot(p.astype(BF16), mv_ref[0].astype(BF16))
    o_ref[0] = jnp.sum(jnp.where(own, o, 0.0), axis=0, keepdims=True)


def _mem_decode(mq, cache_mk, cache_mv, layer):
    bs = mq.shape[0]
    n_mem = cache_mk.shape[1]
    seq = pl.BlockSpec((1, 1, MEM_W), lambda b: (b, 0, 0))
    kv = pl.BlockSpec((1, n_mem, MEM_W), lambda b: (b, 0, layer))
    out = pl.pallas_call(
        _mem_decode_kernel,
        grid=(bs,),
        in_specs=[seq, kv, kv],
        out_specs=seq,
        out_shape=jax.ShapeDtypeStruct((bs, 1, MEM_W), F32),
        compiler_params=_params("parallel"),
        name="mem_decode",
    )(mq[:, None, :], cache_mk, cache_mv)
    return out[:, 0, :]


def _finish_single_kernel(*refs, gated):
    if gated:
        h_ref, a_ref, gate_ref, matt_ref, mg_ref, wo_ref, g_ref, o_ref = refs
        mo = jax.nn.silu(gate_ref[...]) * a_ref[...]
    else:
        h_ref, a_ref, matt_ref, mg_ref, wo_ref, g_ref, o_ref = refs
        mo = a_ref[...]
    ma = jax.nn.silu(mg_ref[...]) * matt_ref[...]
    o_ref[...] = _out_proj_residual(h_ref[...], mo, ma, wo_ref, g_ref)


def _finish_single(h, a, gate, matt, mg, wo, g_post):
    gated = gate is not None
    ins = [h, a] + ([gate] if gated else []) + [matt, mg, wo, g_post]
    full = lambda arr: pl.BlockSpec(arr.shape, lambda: (0,) * arr.ndim)
    return pl.pallas_call(
        functools.partial(_finish_single_kernel, gated=gated),
        in_specs=[full(x) for x in ins],
        out_specs=pl.BlockSpec(h.shape, lambda: (0, 0)),
        out_shape=jax.ShapeDtypeStruct(h.shape, F32),
        compiler_params=pltpu.CompilerParams(vmem_limit_bytes=VMEM_LIMIT),
        name="finish_single",
    )(*ins)


def _rope_tables(pos):
    inv = ROPE_THETA ** (-jnp.arange(ROT_HALF, dtype=F32) / ROT_HALF)
    ang = pos.astype(F32)[:, None] * inv[None, :]
    cos, sin = jnp.cos(ang), jnp.sin(ang)
    n = pos.shape[0]
    rest = DA_HD - ROT_DIM
    c64 = jnp.concatenate([cos, cos, jnp.ones((n, rest), F32)], axis=1)
    s64 = jnp.concatenate([-sin, sin, jnp.zeros((n, rest), F32)], axis=1)
    return jnp.tile(c64, (1, 2)), jnp.tile(s64, (1, 2))


def _pad_lanes(v):
    return jnp.pad(v, (0, LANES - v.shape[0]))[None, :]


def kernel(x_prompt, x_sample, cache_attn_k, cache_attn_v, cache_mem_k, cache_mem_v, state_ssm, state_conv, page_table, mem_prompt, norm_pre, norm_post, w_mem_kv, w_out, attn_w_in, attn_lambda, attn_subln, gm_w_in, gm_norm, gm_ws, gm_bias, ssd_w_in, ssd_conv_w, ssd_conv_b, ssd_dt_bias, ssd_a_log, ssd_d, ssd_norm):
    bp, sp, _ = x_prompt.shape
    bs, ts, _ = x_sample.shape
    assert ts == 1
    depth = norm_pre.shape[0]
    n_mem = mem_prompt.shape[1]
    page = cache_attn_k.shape[1]
    past = page_table.shape[1] * page
    tp = bp * sp
    tm = min(256, sp)
    tq = min(FLASH_TQ, sp)
    tk = tq // 2

    cos_p, sin_p = _rope_tables(jnp.arange(sp, dtype=jnp.int32))
    cos_s, sin_s = _rope_tables(jnp.full((bs,), past, jnp.int32))

    mk_all, mv_all, mkb, mvb = _memkv(mem_prompt.reshape(bp * n_mem, D_MODEL), w_mem_kv)
    mkb = mkb.reshape(depth * bp, n_mem, MEM_W)
    mvb = mvb.reshape(depth * bp, n_mem, MEM_W)
    cache_mk = cache_mem_k.reshape(bs, n_mem, depth * MEM_W)
    cache_mv = cache_mem_v.reshape(bs, n_mem, depth * MEM_W)

    hp = x_prompt.reshape(tp, D_MODEL)
    hs = x_sample.reshape(bs, D_MODEL)
    n_attn = attn_w_in.shape[0]
    kv_p = kv_s = None
    ssm_p, conv_p, ssm_s, conv_s, gm_v_s = [], [], [], [], []
    for i in range(depth):
        kind, j = i % 3, i // 3
        g_pre = norm_pre[i][None, :]
        g_post = norm_post[i][None, :]
        wo = w_out[i].astype(BF16)
        if kind == 0:
            lam_init = 0.8 - 0.6 * math.exp(-0.3 * i)
            w = attn_w_in[j].astype(BF16)
            subln = attn_subln[j][None, :]
            q, k, v, kb, vt, gate_p, mq_p, mg_p = _inproj_attn(hp, g_pre, w, cos_p, sin_p, tm, bp, j, n_attn, kv_p)
            kv_p = (k, v)
            a_p = _flash(attn_lambda[j], q.reshape(bp, sp, MIX_W), kb.reshape(bp, sp, MIX_W), vt,
                         subln, tq, tk, lam_init).reshape(tp, MIX_W)
            q, k, v, gate_s, mq_s, mg_s = _inproj_attn(hs, g_pre, w, cos_s, sin_s, bs, None, j, n_attn, kv_s)
            kv_s = (k, v)
            a_s = _decode_attn(page_table, attn_lambda[j], q, k, v, cache_attn_k, cache_attn_v, subln, j,
                               lam_init).reshape(bs, MIX_W)
        elif kind == 1:
            w = gm_w_in[j].astype(BF16)
            gn = gm_norm[j][None, :]
            segs = lambda mq_dt: ((2 * MIX_W, F32), (MIX_W, F32), (MEM_W, mq_dt), (MEM_W, F32))
            main, gate_p, mq_p, mg_p = _inproj_plain(hp, g_pre, w, tm, segs(BF16), "inproj_gm")
            a_p = _gm_mixer(main, gn, gm_ws[j][:, :min(GM_CHUNK, sp), :min(GM_CHUNK, sp)],
                            gm_bias[j][:, :min(GM_CHUNK, sp)], tm)
            main, gate_s, mq_s, mg_s = _inproj_plain(hs, g_pre, w, bs, segs(F32), "inproj_gm")
            a_s, vn_s = _gm_single(main, gn, gm_ws[j], gm_bias[j])
            gm_v_s.append(vn_s.reshape(bs, ts, MIX_W))
        else:
            wf = ssd_w_in[j]
            n_dt = SSD_HEADS
            tail = SSD_CONV_DIM + n_dt
            w = jnp.concatenate([wf[:, :SSD_CONV_DIM], wf[:, tail:], wf[:, SSD_CONV_DIM:tail],
                                 jnp.zeros((D_MODEL, LANES - n_dt), F32)], axis=1).astype(BF16)
            segs = lambda mq_dt: ((SSD_CONV_DIM, F32), (MIX_W, F32), (MEM_W, mq_dt), (MEM_W, F32), (LANES, F32))
            pars = (ssd_conv_w[j], ssd_conv_b[j][None, :], _pad_lanes(ssd_dt_bias[j]), _pad_lanes(ssd_a_log[j]),
                    jnp.repeat(ssd_d[j], SSD_HD)[None, :], ssd_norm[j][None, :])
            xbc, z, mq_p, mg_p, dt = _inproj_plain(hp, g_pre, w, tm, segs(BF16), "inproj_ssd")
            a_p, st = _ssd_mixer(xbc, dt, z, bp, *pars)
            gate_p = None
            ssm_p.append(st.reshape(bp, SSD_HEADS, SSD_HD, SSD_N))
            conv_p.append(xbc.reshape(bp, sp, SSD_CONV_DIM)[:, sp - (SSD_CONV - 1):, :])
            xbc, z, mq_s, mg_s, dt = _inproj_plain(hs, g_pre, w, bs, segs(F32), "inproj_ssd")
            a_s, cn, sn = _ssd_single(xbc, dt, z, state_conv[j],
                                      state_ssm[j].reshape(bs, SSD_PAIRS, LANES, SSD_N), *pars)
            a_s = a_s.reshape(bs, MIX_W)
            gate_s = None
            ssm_s.append(sn.reshape(bs, SSD_HEADS, SSD_HD, SSD_N))
            conv_s.append(cn)
        hp = _finish_prompt(hp, a_p, gate_p, mq_p, mg_p, mkb, mvb, i, bp, wo, g_post, min(FINISH_TM, sp))
        matt = _mem_decode(mq_s, cache_mk, cache_mv, i)
        hs = _finish_single(hs, a_s, gate_s, matt, mg_s, wo, g_post)

    return (hp.reshape(bp, sp, D_MODEL), hs.reshape(bs, ts, D_MODEL),
            kv_p[0].reshape(bp, sp, n_attn, DA_HEADS, HEAD_W), kv_p[1].reshape(bp, sp, n_attn, DA_HEADS, HEAD_W),
            mk_all.reshape(bp, n_mem, depth, MEM_HEADS, MEM_HD), mv_all.reshape(bp, n_mem, depth, MEM_HEADS, MEM_HD),
            jnp.stack(ssm_p, axis=0), jnp.stack(conv_p, axis=0),
            kv_s[0].reshape(bs, ts, n_attn, DA_HEADS, HEAD_W), kv_s[1].reshape(bs, ts, n_attn, DA_HEADS, HEAD_W),
            jnp.stack(ssm_s, axis=0), jnp.stack(conv_s, axis=0),
            jnp.stack(gm_v_s, axis=2))
```
